```python
import jax, jax.numpy as jnp
from jax import lax
import numpy as np

D_MODEL = 2048
BATCH = 2
SEQ = 16384
DEPTH = 2

CHUNK = 64
EPS = 1e-6
NEG_INF = -1e30

A_HEADS = 6
A_DK = 128
A_DV = 128
A_QF = A_HEADS * A_DK
A_WIDTH = A_HEADS * A_DV

B_HEADS = 5
B_DH = 128
B_WIDTH = B_HEADS * B_DH
B_PREV_CHUNKS = 8
B_BAND = B_PREV_CHUNKS + 1
REL_CLIP = 128

C_HEADS = 5
C_DK = 64
C_DV = 128
C_QK = C_HEADS * C_DK
C_WIDTH = C_HEADS * C_DV
ROPE_BASE = 10000.0

D_MIX = A_WIDTH + B_WIDTH + C_WIDTH
SPLIT_SIZES = (A_QF, A_QF, A_WIDTH, A_WIDTH,
               B_WIDTH, B_WIDTH, B_WIDTH, B_WIDTH,
               C_QK, C_QK, C_WIDTH, C_WIDTH)
D_IN = 2 * A_QF + 2 * A_WIDTH + 4 * B_WIDTH + 2 * C_QK + 2 * C_WIDTH

kernel_name = "hybrid_hgrn2_bandattn_retention"


def rmsnorm(x, gain):
    x32 = x.astype(jnp.float32)
    y = x32 * lax.rsqrt(jnp.mean(x32 * x32, axis=-1, keepdims=True) + EPS)
    return (y * gain.astype(jnp.float32)).astype(x.dtype)


def split_columns(t):
    parts, start = [], 0
    for size in SPLIT_SIZES:
        parts.append(t[..., start:start + size])
        start += size
    return parts


def hgrn2_group(q, f_logit, i, lb, gain):
    f32 = jnp.float32
    bsz, s, _ = q.shape
    n = s // CHUNK
    lb = lb.astype(f32)
    z = f_logit.astype(f32)
    log_f = jnp.logaddexp(jnp.log(lb), jnp.log1p(-lb) + jax.nn.log_sigmoid(z))
    k = (1.0 - lb) * jax.nn.sigmoid(-z)
    q = jax.nn.silu(q.astype(f32)) * (A_DK ** -0.5)

    def to_chunks(t, d):
        return t.reshape(bsz, n, CHUNK, A_HEADS, d).transpose(1, 0, 3, 2, 4)

    qc, kc, gc = to_chunks(q, A_DK), to_chunks(k, A_DK), to_chunks(log_f, A_DK)
    vc = to_chunks(i.astype(f32), A_DV)
    causal = jnp.tril(jnp.ones((CHUNK, CHUNK), dtype=bool))

    def step(state, inp):
        qn, kn, vn, gn = inp
        b = jnp.cumsum(gn, axis=2)
        b_last = b[:, :, -1:, :]
        o_inter = jnp.einsum('bhtk,bhkv->bhtv', qn * jnp.exp(b), state)
        diff = b[:, :, :, None, :] - b[:, :, None, :, :]
        decay = jnp.exp(jnp.where(causal[:, :, None], diff, -jnp.inf))
        scores = jnp.einsum('bhtsk,bhsk->bhts', qn[:, :, :, None, :] * decay, kn)
        o = o_inter + jnp.einsum('bhts,bhsv->bhtv', scores, vn)
        new_state = (jnp.exp(b_last[:, :, 0, :])[..., None] * state
                     + jnp.einsum('bhsk,bhsv->bhkv', kn * jnp.exp(b_last - b), vn))
        return new_state, o

    s0 = jnp.zeros((bsz, A_HEADS, A_DK, A_DV), f32)
    _, o = lax.scan(step, s0, (qc, kc, vc, gc))
    o = o * lax.rsqrt(jnp.mean(o * o, axis=-1, keepdims=True) + EPS) * gain.astype(f32)
    return o.transpose(1, 0, 3, 2, 4).reshape(bsz, s, A_WIDTH)


def band_attention_group(q, k, v, rel_bias):
    f32 = jnp.float32
    bsz, s, _ = q.shape
    n = s // CHUNK

    def to_chunks(t):
        return t.reshape(bsz, n, CHUNK, B_HEADS, B_DH).transpose(0, 3, 1, 2, 4)

    qc, kc, vc = to_chunks(q), to_chunks(k), to_chunks(v)
    pad = ((0, 0), (0, 0), (B_PREV_CHUNKS, 0), (0, 0), (0, 0))
    band_idx = jnp.arange(n)[:, None] + jnp.arange(B_BAND)[None, :]
    k_band = jnp.pad(kc, pad)[:, :, band_idx].reshape(bsz, B_HEADS, n, B_BAND * CHUNK, B_DH)
    v_band = jnp.pad(vc, pad)[:, :, band_idx].reshape(bsz, B_HEADS, n, B_BAND * CHUNK, B_DH)

    q_pos = jnp.arange(CHUNK)
    k_pos = jnp.arange(B_BAND * CHUNK) - B_PREV_CHUNKS * CHUNK
    rel = q_pos[:, None] - k_pos[None, :]
    bias = rel_bias[:, jnp.clip(rel, -REL_CLIP, REL_CLIP) + REL_CLIP].astype(f32)
    valid = (jnp.arange(n)[:, None] + jnp.arange(B_BAND * CHUNK)[None, :] // CHUNK) >= B_PREV_CHUNKS

    scores = jnp.einsum('bhnqd,bhnkd->bhnqk', qc, k_band, preferred_element_type=f32)
    scores = scores * (B_DH ** -0.5) + bias[None, :, None]
    scores = jnp.where(valid[None, None, :, None, :], scores, NEG_INF)
    p = jax.nn.softmax(scores, axis=-1)
    o = jnp.einsum('bhnqk,bhnkd->bhnqd', p.astype(v_band.dtype), v_band, preferred_element_type=f32)
    return o.transpose(0, 2, 3, 1, 4).reshape(bsz, s, B_WIDTH)


def apply_rotary(t, cos, sin):
    t1, t2 = t[..., :C_DK // 2], t[..., C_DK // 2:]
    c, s = cos[None, :, None, :], sin[None, :, None, :]
    return jnp.concatenate([t1 * c - t2 * s, t1 * s + t2 * c], axis=-1)


def retention_group(q, k, v, cos, sin):
    f32 = jnp.float32
    bsz, s, _ = q.shape
    n = s // CHUNK
    q = apply_rotary(q.astype(f32).reshape(bsz, s, C_HEADS, C_DK), cos, sin)
    k = apply_rotary(k.astype(f32).reshape(bsz, s, C_HEADS, C_DK), cos, sin) * (C_DK ** -0.5)

    def to_chunks(t, d):
        return t.reshape(bsz, n, CHUNK, C_HEADS, d).transpose(0, 3, 1, 2, 4)

    qc, kc = to_chunks(q, C_DK), to_chunks(k, C_DK)
    vc = to_chunks(v.astype(f32).reshape(bsz, s, C_HEADS, C_DV), C_DV)

    log_gamma = jnp.log1p(-jnp.exp2(-5.0 - jnp.arange(C_HEADS, dtype=f32)))
    pos = jnp.arange(CHUNK, dtype=f32)
    rel = pos[:, None] - pos[None, :]
    decay_mask = jnp.where(rel >= 0, jnp.exp(log_gamma[:, None, None] * jnp.maximum(rel, 0.0)), 0.0)

    scores = jnp.einsum('bhnik,bhnjk->bhnij', qc, kc) * decay_mask[None, :, None]
    o_intra = jnp.einsum('bhnij,bhnjv->bhniv', scores, vc)

    k_dec = kc * jnp.exp(log_gamma[:, None] * (CHUNK - 1 - pos)[None, :])[None, :, None, :, None]
    d_state = jnp.einsum('bhnjk,bhnjv->bhnkv', k_dec, vc)
    chunk_decay = jnp.exp(log_gamma * CHUNK)[None, :, None, None]

    def step(state, ds):
        return chunk_decay * state + ds, state

    s0 = jnp.zeros((bsz, C_HEADS, C_DK, C_DV), f32)
    _, prev_state = lax.scan(step, s0, d_state.transpose(2, 0, 1, 3, 4))
    prev_state = prev_state.transpose(1, 2, 0, 3, 4)
    q_dec = qc * jnp.exp(log_gamma[:, None] * (pos + 1.0)[None, :])[None, :, None, :, None]
    o = o_intra + jnp.einsum('bhnik,bhnkv->bhniv', q_dec, prev_state)
    o = o * lax.rsqrt(jnp.mean(o * o, axis=-1, keepdims=True) + EPS)
    return o.transpose(0, 2, 3, 1, 4).reshape(bsz, s, C_WIDTH)


def hybrid_layer(x, w_in, norm_gain, lb, hgrn_gain, rel_bias, w_out, cos, sin):
    h = rmsnorm(x, norm_gain)
    proj = jnp.einsum('bsd,de->bse', h, w_in)
    aq, af, ai, ag, bq, bk, bv, bg, cq, ck, cv, cg = split_columns(proj)
    ya = hgrn2_group(aq, af, ai, lb, hgrn_gain).astype(x.dtype) * jax.nn.silu(ag)
    yb = band_attention_group(bq, bk, bv, rel_bias).astype(x.dtype) * jax.nn.silu(bg)
    yc = retention_group(cq, ck, cv, cos, sin).astype(x.dtype) * jax.nn.silu(cg)
    y = jnp.concatenate([ya, yb, yc], axis=-1)
    return x + jnp.einsum('bse,ed->bsd', y, w_out)


def setup_inputs(seed: int = 0) -> dict:
    key = jax.random.key(seed)
    ks = jax.random.split(key, 8)
    x = jax.random.normal(ks[0], (BATCH, SEQ, D_MODEL), jnp.float32)
    w_in = jax.random.normal(ks[1], (DEPTH, D_MODEL, D_IN), jnp.float32) * D_MODEL ** -0.5
    norm_gain = 1.0 + 0.05 * jax.random.normal(ks[2], (DEPTH, D_MODEL), jnp.float32)
    lb_logits = 0.5 * jax.random.normal(ks[3], (DEPTH, A_QF), jnp.float32)
    hgrn_norm_gain = 1.0 + 0.05 * jax.random.normal(ks[4], (DEPTH, A_DV), jnp.float32)
    rel_bias = 0.1 * jax.random.normal(ks[5], (DEPTH, B_HEADS, 2 * REL_CLIP + 1), jnp.float32)
    w_out = jax.random.normal(ks[6], (DEPTH, D_MIX, D_MODEL), jnp.float32) * D_MIX ** -0.5
    final_gain = 1.0 + 0.05 * jax.random.normal(ks[7], (D_MODEL,), jnp.float32)
    return {"x": x, "w_in": w_in, "norm_gain": norm_gain, "lb_logits": lb_logits,
            "hgrn_norm_gain": hgrn_norm_gain, "rel_bias": rel_bias, "w_out": w_out,
            "final_gain": final_gain}


def reference(x, w_in, norm_gain, lb_logits, hgrn_norm_gain, rel_bias, w_out, final_gain):
    s = x.shape[1]
    lb_all = jnp.cumsum(jax.nn.softmax(lb_logits.astype(jnp.float32), axis=0), axis=0)
    lb_all = lb_all - lb_all[0:1]
    inv_freq = ROPE_BASE ** (-jnp.linspace(0.0, 1.0, C_DK // 2, dtype=jnp.float32))
    ang = jnp.arange(s, dtype=jnp.float32)[:, None] * inv_freq[None, :]
    cos, sin = jnp.cos(ang), jnp.sin(ang)
    for layer in range(DEPTH):
        x = hybrid_layer(x, w_in[layer], norm_gain[layer], lb_all[layer], hgrn_norm_gain[layer],
                         rel_bias[layer], w_out[layer], cos, sin)
    return rmsnorm(x, final_gain)
```

```python
import functools

import numpy as np
import jax
import jax.numpy as jnp
from jax import lax
from jax.experimental import pallas as pl
from jax.experimental.pallas import tpu as pltpu

F32 = jnp.float32
BF16 = jnp.bfloat16

D_MODEL = 2048
DEPTH = 2
CHUNK = 64
EPS = 1e-6
NEG_INF = -1e30
LANE = 128

A_HEADS, A_D = 6, 128
A_WIDTH = A_HEADS * A_D
B_HEADS, B_DH = 5, 128
B_WIDTH = B_HEADS * B_DH
B_PREV = 8
REL_CLIP = 128
C_HEADS, C_DK, C_DV = 5, 64, 128
C_QK = C_HEADS * C_DK
C_QK_PAD = 384
C_WIDTH = C_HEADS * C_DV
ROPE_BASE = 10000.0
D_MIX = A_WIDTH + B_WIDTH + C_WIDTH
D_IN_PAD = 7680
N_SLOTS = D_IN_PAD // LANE

SLOT_BQ, SLOT_BK, SLOT_BV, SLOT_BG = 0, 1, 2, 3
SLOT_CV, SLOT_CG = 4, 5
SLOT_CQ, SLOT_CK = 10, 11
SLOT_AQ, SLOT_AF, SLOT_AI, SLOT_AG = 6, 7, 8, 9

TM_NORM = 512
TM_PROJ = 1024
TN_PROJ = 1536
TN_SUB = 512
TM_OUT = 512
TA = 512
TB = 256
TC = 256
A_LEVELS = (1, 2, 4, 8, 16, 32)
VMEM_LIMIT = 56 * 1024 * 1024


def _cparams(sem):
    return pltpu.CompilerParams(dimension_semantics=sem, vmem_limit_bytes=VMEM_LIMIT)


def _prenorm_kernel(x_ref, g_ref, h_ref):
    x = x_ref[...]
    ms = jnp.mean(x * x, axis=-1, keepdims=True)
    h_ref[...] = (x * lax.rsqrt(ms + EPS) * g_ref[...]).astype(BF16)


def _prenorm(x2, gain):
    m = x2.shape[0]
    return pl.pallas_call(
        _prenorm_kernel,
        grid=(m // TM_NORM,),
        in_specs=[pl.BlockSpec((TM_NORM, D_MODEL), lambda i: (i, 0)),
                  pl.BlockSpec((1, D_MODEL), lambda i: (0, 0))],
        out_specs=pl.BlockSpec((TM_NORM, D_MODEL), lambda i: (i, 0)),
        out_shape=jax.ShapeDtypeStruct((m, D_MODEL), BF16),
        compiler_params=_cparams(("arbitrary",)),
        name="prenorm",
    )(x2, gain.reshape(1, D_MODEL))


def _inproj_kernel(h_ref, w_ref, o_ref):
    h = h_ref[...]
    for c in range(0, TN_PROJ, TN_SUB):
        acc = jnp.dot(h, w_ref[:, c:c + TN_SUB], preferred_element_type=F32)
        for k in range(TN_SUB // LANE):
            o_ref[c // LANE + k] = acc[:, k * LANE:(k + 1) * LANE].astype(BF16)


def _inproj(h, w):
    m = h.shape[0]
    spb = TN_PROJ // LANE
    return pl.pallas_call(
        _inproj_kernel,
        grid=(m // TM_PROJ, D_IN_PAD // TN_PROJ),
        in_specs=[pl.BlockSpec((TM_PROJ, D_MODEL), lambda i, j: (i, 0)),
                  pl.BlockSpec((D_MODEL, TN_PROJ), lambda i, j: (0, j))],
        out_specs=pl.BlockSpec((spb, TM_PROJ, LANE), lambda i, j: (j, i, 0)),
        out_shape=jax.ShapeDtypeStruct((N_SLOTS, m, LANE), BF16),
        compiler_params=_cparams(("arbitrary", "arbitrary")),
        name="inproj",
    )(h, w)


def _sigmoid(x):
    return 1.0 / (1.0 + jnp.exp(-x))


def _row(x, r):
    return jnp.broadcast_to(x[r:r + 1, :], (8, x.shape[1]))


def _decay_products(f):
    w = f.shape[1]
    r = lax.broadcasted_iota(jnp.int32, (8, w), 0)
    b0, b1, b2 = (r & 1) == 1, (r & 2) == 2, (r & 4) == 4
    lo = r < 4
    one = jnp.ones((8, w), F32)
    groups = [f[8 * j:8 * j + 8, :] for j in range(8)]
    p = {1: groups, 2: [], 4: [], 8: []}
    rr = {1: [one] * 8, 2: [], 4: [], 8: []}
    for x1 in groups:
        x2 = x1 * jnp.where(b0, pltpu.roll(x1, 1, 0), one)
        x4 = x2 * jnp.where(b1, jnp.where(lo, _row(x2, 1), _row(x2, 5)), one)
        x8 = x4 * jnp.where(b2, _row(x4, 3), one)
        r2 = jnp.where(b0, one, pltpu.roll(x1, 7, 0))
        r4 = r2 * jnp.where(b1, one, jnp.where(lo, _row(x2, 3), _row(x2, 7)))
        r8 = r4 * jnp.where(b2, one, _row(x4, 7))
        p[2].append(x2); p[4].append(x4); p[8].append(x8)
        rr[2].append(r2); rr[4].append(r4); rr[8].append(r8)
    for m in (8, 16, 32):
        g = m // 8
        tot = [_row(p[m][g * b + g - 1], 7) for b in range(64 // m)]
        pn, rn = [], []
        for j in range(8):
            b = j // g
            pn.append(p[m][j] * tot[b - 1] if b % 2 == 1 else p[m][j])
            rn.append(rr[m][j] * tot[b + 1] if b % 2 == 0 else rr[m][j])
        p[2 * m], rr[2 * m] = pn, rn
    cat = lambda xs: jnp.concatenate(xs, axis=0)
    return {m: (cat(p[m]), cat(rr[m])) for m in (1, 2, 4, 8, 16, 32, 64)}


def _hgrn_kernel(q_ref, f_ref, i_ref, g_ref, lb_ref, gain_ref, mask_ref, y_ref, st_ref):
    @pl.when(pl.program_id(1) == 0)
    def _():
        st_ref[...] = jnp.zeros_like(st_ref)

    lb = lb_ref[...]
    oml = 1.0 - lb
    gain = gain_ref[...]
    nt = (((1,), (1,)), ((), ()))
    tn = (((0,), (0,)), ((), ()))

    def chunk(c, carry):
        r0 = pl.multiple_of(c * CHUNK, CHUNK)

        def load(ref):
            return jnp.concatenate(
                [ref[h, pl.ds(r0, CHUNK), :] for h in range(A_HEADS)], axis=-1).astype(F32)

        z, qq, vv, gg = load(f_ref), load(q_ref), load(i_ref), load(g_ref)
        e = jnp.exp(-jnp.abs(z))
        rcp = 1.0 / (1.0 + e)
        er = e * rcp
        pos = z >= 0.0
        f = lb + oml * jnp.where(pos, rcp, er)
        kk = oml * jnp.where(pos, er, rcp)
        qs = qq * _sigmoid(qq) * (A_D ** -0.5)
        gate = gg * _sigmoid(gg)
        prod = _decay_products(f)
        qlev = [(qs * prod[m][0]).astype(BF16) for m in A_LEVELS]
        klev = [(kk * prod[m][1]).astype(BF16) for m in A_LEVELS]
        q_in = (qs * prod[64][0]).astype(BF16)
        k_out = (kk * prod[64][1]).astype(BF16)
        ftot = prod[64][0][CHUNK - 1:CHUNK, :]
        qk = qs * kk
        vb = vv.astype(BF16)
        for h in range(A_HEADS):
            sl = slice(h * A_D, (h + 1) * A_D)
            s_tot = jnp.zeros((CHUNK, CHUNK), F32)
            for li in range(len(A_LEVELS)):
                s = lax.dot_general(qlev[li][:, sl], klev[li][:, sl], nt,
                                    preferred_element_type=F32)
                s_tot = s_tot + s * mask_ref[li]
            st = st_ref[h]
            o = jnp.dot(s_tot.astype(BF16), vb[:, sl], preferred_element_type=F32)
            o = o + lax.dot_general(q_in[:, sl], st.astype(BF16), nt,
                                    preferred_element_type=F32)
            o = o + jnp.sum(qk[:, sl], axis=-1, keepdims=True) * vv[:, sl]
            st_ref[h] = st * ftot[:, sl] + lax.dot_general(
                vb[:, sl], k_out[:, sl], tn, preferred_element_type=F32)
            o = o * lax.rsqrt(jnp.mean(o * o, axis=-1, keepdims=True) + EPS) * gain
            y_ref[pl.ds(r0, CHUNK), sl] = (o * gate[:, sl]).astype(BF16)
        return carry

    lax.fori_loop(0, TA // CHUNK, chunk, 0)


def _hgrn_masks():
    t = np.arange(CHUNK)[:, None]
    s = np.arange(CHUNK)[None, :]
    out = []
    for m in A_LEVELS:
        ok = (t // (2 * m) == s // (2 * m)) & ((t // m) % 2 == 1) & ((s // m) % 2 == 0)
        out.append(ok)
    return np.stack(out).astype(np.float32)


def _hgrn(proj, lb, gain, bsz, seq):
    m = bsz * seq
    nb = seq // TA
    spec = lambda blk: pl.BlockSpec((A_HEADS, TA, LANE), lambda b, i, blk=blk: (blk, b * nb + i, 0))
    return pl.pallas_call(
        _hgrn_kernel,
        grid=(bsz, nb),
        in_specs=[spec(SLOT_AQ), spec(SLOT_AF), spec(SLOT_AI), spec(SLOT_AG),
                  pl.BlockSpec((1, A_WIDTH), lambda b, i: (0, 0)),
                  pl.BlockSpec((1, A_D), lambda b, i: (0, 0)),
                  pl.BlockSpec((len(A_LEVELS), CHUNK, CHUNK), lambda b, i: (0, 0, 0))],
        out_specs=pl.BlockSpec((TA, A_WIDTH), lambda b, i: (b * nb + i, 0)),
        out_shape=jax.ShapeDtypeStruct((m, A_WIDTH), BF16),
        scratch_shapes=[pltpu.VMEM((A_HEADS, A_D, A_D), F32)],
        compiler_params=_cparams(("arbitrary", "arbitrary")),
        name="hgrn",
    )(proj, proj, proj, proj, lb.reshape(1, A_WIDTH), gain.reshape(1, A_D),
      jnp.asarray(_hgrn_masks()))


def _band_kernel(q_ref, k0_ref, k1_ref, k2_ref, v0_ref, v1_ref, v2_ref, g_ref, bias_ref, y_ref):
    i = pl.program_id(1)
    nt = (((1,), (1,)), ((), ()))
    pen0 = jnp.where(i >= 2, 0.0, NEG_INF).astype(F32)
    pen1 = jnp.where(i >= 1, 0.0, NEG_INF).astype(F32)
    scale = B_DH ** -0.5
    for h in range(B_HEADS):
        q = q_ref[h]
        s0 = lax.dot_general(q, k0_ref[h], nt, preferred_element_type=F32)
        s1 = lax.dot_general(q, k1_ref[h], nt, preferred_element_type=F32)
        s2 = lax.dot_general(q, k2_ref[h], nt, preferred_element_type=F32)
        s0 = s0 * scale + bias_ref[h, :, 0:TB] + pen0
        s1 = s1 * scale + bias_ref[h, :, TB:2 * TB] + pen1
        s2 = s2 * scale + bias_ref[h, :, 2 * TB:3 * TB]
        mx = jnp.maximum(jnp.maximum(jnp.max(s0, axis=-1, keepdims=True),
                                     jnp.max(s1, axis=-1, keepdims=True)),
                         jnp.max(s2, axis=-1, keepdims=True))
        p0, p1, p2 = jnp.exp(s0 - mx), jnp.exp(s1 - mx), jnp.exp(s2 - mx)
        den = (jnp.sum(p0, axis=-1, keepdims=True) + jnp.sum(p1, axis=-1, keepdims=True)
               + jnp.sum(p2, axis=-1, keepdims=True))
        o = (jnp.dot(p0.astype(BF16), v0_ref[h], preferred_element_type=F32)
             + jnp.dot(p1.astype(BF16), v1_ref[h], preferred_element_type=F32)
             + jnp.dot(p2.astype(BF16), v2_ref[h], preferred_element_type=F32))
        g = g_ref[h].astype(F32)
        y_ref[:, h * B_DH:(h + 1) * B_DH] = (o / den * (g * _sigmoid(g))).astype(BF16)


def _band_bias(rel_bias):
    qr = np.arange(TB)[:, None]
    kc = np.arange(3 * TB)[None, :]
    jj = kc // CHUNK - qr // CHUNK
    inband = (jj >= 0) & (jj <= B_PREV)
    rel = qr % CHUNK - ((jj - B_PREV) * CHUNK + kc % CHUNK)
    idx = np.clip(rel, -REL_CLIP, REL_CLIP) + REL_CLIP
    tab = rel_bias.astype(F32)[:, idx]
    return jnp.where(jnp.asarray(inband)[None], tab, NEG_INF)


def _band(proj, bias, bsz, seq):
    m = bsz * seq
    nb = seq // TB
    cur = lambda blk: pl.BlockSpec((B_HEADS, TB, LANE), lambda b, i, blk=blk: (blk, b * nb + i, 0))
    prev = lambda blk, d: pl.BlockSpec(
        (B_HEADS, TB, LANE), lambda b, i, blk=blk, d=d: (blk, b * nb + jnp.maximum(i - d, 0), 0))
    return pl.pallas_call(
        _band_kernel,
        grid=(bsz, nb),
        in_specs=[cur(SLOT_BQ),
                  prev(SLOT_BK, 2), prev(SLOT_BK, 1), cur(SLOT_BK),
                  prev(SLOT_BV, 2), prev(SLOT_BV, 1), cur(SLOT_BV),
                  cur(SLOT_BG),
                  pl.BlockSpec((B_HEADS, TB, 3 * TB), lambda b, i: (0, 0, 0))],
        out_specs=pl.BlockSpec((TB, B_WIDTH), lambda b, i: (b * nb + i, 0)),
        out_shape=jax.ShapeDtypeStruct((m, B_WIDTH), BF16),
        compiler_params=_cparams(("arbitrary", "arbitrary")),
        name="band",
    )(proj, proj, proj, proj, proj, proj, proj, proj, bias)


def _ret_kernel(q_ref, k_ref, v_ref, g_ref, cos_ref, sin_ref, dmask_ref, qdec_ref, kdec_ref,
                kin_ref, cdec_ref, y_ref, st_ref):
    @pl.when(pl.program_id(1) == 0)
    def _():
        st_ref[...] = jnp.zeros_like(st_ref)

    nt = (((1,), (1,)), ((), ()))
    tn = (((0,), (0,)), ((), ()))
    cos = cos_ref[...]
    sin = sin_ref[...]
    lane = lax.broadcasted_iota(jnp.int32, (TC, LANE), 1)
    first_half = (lane % C_DK) < (C_DK // 2)

    def rotary(x):
        swapped = jnp.where(first_half, pltpu.roll(x, LANE - C_DK // 2, 1),
                            pltpu.roll(x, C_DK // 2, 1))
        return x * cos + swapped * sin

    qrot = [rotary(q_ref[t].astype(F32)) for t in range(C_QK_PAD // LANE)]
    krot = [rotary(k_ref[t].astype(F32)) for t in range(C_QK_PAD // LANE)]
    for h in range(C_HEADS):
        qp, kp = qrot[h // 2], krot[h // 2]
        v = v_ref[h]
        s = lax.dot_general(qp.astype(BF16), (kp * kin_ref[h]).astype(BF16), nt,
                            preferred_element_type=F32)
        s = s * dmask_ref[h]
        st = st_ref[h]
        o = jnp.dot(s.astype(BF16), v, preferred_element_type=F32)
        o = o + jnp.dot((qp * qdec_ref[h]).astype(BF16), st.astype(BF16),
                        preferred_element_type=F32)
        st_ref[h] = st * cdec_ref[h] + lax.dot_general(
            (kp * kdec_ref[h]).astype(BF16), v, tn, preferred_element_type=F32)
        o = o * lax.rsqrt(jnp.mean(o * o, axis=-1, keepdims=True) + EPS)
        g = g_ref[h].astype(F32)
        y_ref[:, h * C_DV:(h + 1) * C_DV] = (o * (g * _sigmoid(g))).astype(BF16)


def _ret_consts():
    hh = np.arange(C_HEADS, dtype=np.float64)
    log_gamma = np.log1p(-np.exp2(-5.0 - hh))
    pos = np.arange(TC, dtype=np.float64)
    rel = pos[:, None] - pos[None, :]
    dmask = np.where(rel >= 0, np.exp(log_gamma[:, None, None] * np.maximum(rel, 0.0)), 0.0)
    lane = np.arange(LANE)
    head_lanes = np.stack([(lane // C_DK) == (h % 2) for h in range(C_HEADS)]).astype(np.float64)
    qdec = np.exp(log_gamma[:, None] * (pos + 1.0)[None, :])[:, :, None] * head_lanes[:, None, :]
    kdec = (np.exp(log_gamma[:, None] * (TC - 1 - pos)[None, :])[:, :, None]
            * head_lanes[:, None, :] * C_DK ** -0.5)
    kin = head_lanes[:, None, :] * C_DK ** -0.5
    cdec = np.broadcast_to(np.exp(log_gamma * TC)[:, None, None], (C_HEADS, 1, C_DV))
    f = lambda a: jnp.asarray(np.ascontiguousarray(a), dtype=F32)
    return f(dmask), f(qdec), f(kdec), f(kin), f(cdec)


def _rope_tables(seq):
    inv_freq = ROPE_BASE ** (-jnp.linspace(0.0, 1.0, C_DK // 2, dtype=F32))
    ang = jnp.arange(seq, dtype=F32)[:, None] * inv_freq[None, :]
    cos, sin = jnp.cos(ang), jnp.sin(ang)
    cos_t = jnp.tile(cos, (1, LANE // (C_DK // 2)))
    sin_t = jnp.tile(jnp.concatenate([-sin, sin], axis=1), (1, LANE // C_DK))
    return cos_t, sin_t


def _ret(proj, cos_t, sin_t, consts, bsz, seq):
    m = bsz * seq
    nb = seq // TC
    dmask, qdec, kdec, kin, cdec = consts
    spec = lambda n, blk: pl.BlockSpec((n, TC, LANE), lambda b, i, blk=blk: (blk, b * nb + i, 0))
    const = lambda shape: pl.BlockSpec(shape, lambda b, i: (0,) * len(shape))
    return pl.pallas_call(
        _ret_kernel,
        grid=(bsz, nb),
        in_specs=[spec(3, SLOT_CQ), spec(3, SLOT_CK), spec(5, SLOT_CV), spec(5, SLOT_CG),
                  pl.BlockSpec((TC, LANE), lambda b, i: (i, 0)),
                  pl.BlockSpec((TC, LANE), lambda b, i: (i, 0)),
                  const(dmask.shape), const(qdec.shape), const(kdec.shape),
                  const(kin.shape), const(cdec.shape)],
        out_specs=pl.BlockSpec((TC, C_WIDTH), lambda b, i: (b * nb + i, 0)),
        out_shape=jax.ShapeDtypeStruct((m, C_WIDTH), BF16),
        scratch_shapes=[pltpu.VMEM((C_HEADS, LANE, C_DV), F32)],
        compiler_params=_cparams(("arbitrary", "arbitrary")),
        name="retention",
    )(proj, proj, proj, proj, cos_t, sin_t, dmask, qdec, kdec, kin, cdec)


def _outproj_kernel(ya_ref, yb_ref, yc_ref, x_ref, w_ref, g_ref, *out_refs, last):
    y = jnp.concatenate([ya_ref[...], yb_ref[...], yc_ref[...]], axis=-1)
    xn = x_ref[...] + jnp.dot(y, w_ref[...], preferred_element_type=F32)
    ms = jnp.mean(xn * xn, axis=-1, keepdims=True)
    hn = xn * lax.rsqrt(ms + EPS) * g_ref[...]
    if last:
        out_refs[0][...] = hn
    else:
        out_refs[0][...] = xn
        out_refs[1][...] = hn.astype(BF16)


def _outproj(ya, yb, yc, x2, w, gain, last):
    m = x2.shape[0]
    row = lambda width: pl.BlockSpec((TM_OUT, width), lambda i: (i, 0))
    if last:
        out_specs = row(D_MODEL)
        out_shape = jax.ShapeDtypeStruct((m, D_MODEL), F32)
    else:
        out_specs = [row(D_MODEL), row(D_MODEL)]
        out_shape = [jax.ShapeDtypeStruct((m, D_MODEL), F32),
                     jax.ShapeDtypeStruct((m, D_MODEL), BF16)]
    return pl.pallas_call(
        functools.partial(_outproj_kernel, last=last),
        grid=(m // TM_OUT,),
        in_specs=[row(A_WIDTH), row(B_WIDTH), row(C_WIDTH), row(D_MODEL),
                  pl.BlockSpec((D_MIX, D_MODEL), lambda i: (0, 0)),
                  pl.BlockSpec((1, D_MODEL), lambda i: (0, 0))],
        out_specs=out_specs,
        out_shape=out_shape,
        compiler_params=_cparams(("arbitrary",)),
        name="outproj_last" if last else "outproj",
    )(ya, yb, yc, x2, w, gain.reshape(1, D_MODEL))


def _permute_w_in(w):
    aq, af, ai, ag = (w[:, k * A_WIDTH:(k + 1) * A_WIDTH] for k in range(4))
    o = 4 * A_WIDTH
    bq, bk, bv, bg = (w[:, o + k * B_WIDTH:o + (k + 1) * B_WIDTH] for k in range(4))
    o += 4 * B_WIDTH
    cq, ck = w[:, o:o + C_QK], w[:, o + C_QK:o + 2 * C_QK]
    o += 2 * C_QK
    cv, cg = w[:, o:o + C_WIDTH], w[:, o + C_WIDTH:o + 2 * C_WIDTH]
    pad = jnp.zeros((w.shape[0], C_QK_PAD - C_QK), w.dtype)
    return jnp.concatenate([bq, bk, bv, bg, cv, cg, cq, pad, ck, pad, aq, af, ai, ag],
                           axis=1).astype(BF16)


def kernel(x, w_in, norm_gain, lb_logits, hgrn_norm_gain, rel_bias, w_out, final_gain):
    bsz, seq, _ = x.shape
    m = bsz * seq
    lb_all = jnp.cumsum(jax.nn.softmax(lb_logits.astype(F32), axis=0), axis=0)
    lb_all = lb_all - lb_all[0:1]
    cos_t, sin_t = _rope_tables(seq)
    ret_consts = _ret_consts()
    x2 = x.reshape(m, D_MODEL)
    h = _prenorm(x2, norm_gain[0])
    for layer in range(DEPTH):
        proj = _inproj(h, _permute_w_in(w_in[layer]))
        ya = _hgrn(proj, lb_all[layer], hgrn_norm_gain[layer], bsz, seq)
        yb = _band(proj, _band_bias(rel_bias[layer]), bsz, seq)
        yc = _ret(proj, cos_t, sin_t, ret_consts, bsz, seq)
        last = layer == DEPTH - 1
        gain = final_gain if last else norm_gain[layer + 1]
        res = _outproj(ya, yb, yc, x2, w_out[layer].astype(BF16), gain, last)
        if last:
            x2 = res
        else:
            x2, h = res
    return x2.reshape(bsz, seq, D_MODEL)
```

```python
import functools

import numpy as np
import jax
import jax.numpy as jnp
from jax import lax
from jax.experimental import pallas as pl
from jax.experimental.pallas import tpu as pltpu

F32 = jnp.float32
BF16 = jnp.bfloat16

D_MODEL = 2048
DEPTH = 2
CHUNK = 64
EPS = 1e-6
NEG_INF = -1e30
LOG2E = 1.4426950408889634
LANE = 128

A_HEADS, A_D = 6, 128
A_WIDTH = A_HEADS * A_D
B_HEADS, B_DH = 5, 128
B_WIDTH = B_HEADS * B_DH
B_PREV = 8
REL_CLIP = 128
C_HEADS, C_DK, C_DV = 5, 64, 128
C_QK = C_HEADS * C_DK
C_WIDTH = C_HEADS * C_DV
ROPE_BASE = 10000.0
D_MIX = A_WIDTH + B_WIDTH + C_WIDTH
D_IN_PAD = 7680
N_SLOTS = D_IN_PAD // LANE

SLOT_AQ, SLOT_AF, SLOT_AI, SLOT_AG = 0, 1, 2, 3
SLOT_BQ, SLOT_BK, SLOT_BV, SLOT_BG = 5, 6, 7, 8
SLOT_CQK, SLOT_CV, SLOT_CG = 9, 10, 11
C_QK_TILES = 2 * C_QK // LANE

TM_PROJ = 1024
TN_PROJ = 1536
TN_SUB = 512
TM_OUT = 512
TA = 512
TB = 256
TC = 256
A_LEVELS = (1, 2, 4, 8, 16, 32)
VMEM_LIMIT = 56 * 1024 * 1024


def _cparams(sem):
    return pltpu.CompilerParams(dimension_semantics=sem, vmem_limit_bytes=VMEM_LIMIT)


def _project(h, w_ref, o_ref):
    for c in range(0, TN_PROJ, TN_SUB):
        acc = jnp.dot(h, w_ref[:, c:c + TN_SUB], preferred_element_type=F32)
        for k in range(TN_SUB // LANE):
            o_ref[c // LANE + k] = acc[:, k * LANE:(k + 1) * LANE].astype(BF16)


def _inproj_kernel(h_ref, w_ref, o_ref):
    _project(h_ref[...], w_ref, o_ref)


def _inproj_norm_kernel(x_ref, g_ref, w_ref, o_ref, h_ref):
    @pl.when(pl.program_id(1) == 0)
    def _():
        x = x_ref[...]
        ms = jnp.mean(x * x, axis=-1, keepdims=True)
        h_ref[...] = (x * lax.rsqrt(ms + EPS) * g_ref[...]).astype(BF16)

    _project(h_ref[...], w_ref, o_ref)


def _inproj(h, w, gain=None):
    m = h.shape[0]
    spb = TN_PROJ // LANE
    rows = pl.BlockSpec((TM_PROJ, D_MODEL), lambda i, j: (i, 0))
    wspec = pl.BlockSpec((D_MODEL, TN_PROJ), lambda i, j: (0, j))
    common = dict(
        grid=(m // TM_PROJ, D_IN_PAD // TN_PROJ),
        out_specs=pl.BlockSpec((spb, TM_PROJ, LANE), lambda i, j: (j, i, 0)),
        out_shape=jax.ShapeDtypeStruct((N_SLOTS, m, LANE), BF16),
        compiler_params=_cparams(("arbitrary", "arbitrary")),
    )
    if gain is None:
        return pl.pallas_call(_inproj_kernel, in_specs=[rows, wspec], name="inproj",
                              **common)(h, w)
    return pl.pallas_call(
        _inproj_norm_kernel,
        in_specs=[rows, pl.BlockSpec((1, D_MODEL), lambda i, j: (0, 0)), wspec],
        scratch_shapes=[pltpu.VMEM((TM_PROJ, D_MODEL), BF16)],
        name="inproj_norm", **common)(h, gain.reshape(1, D_MODEL), w)


def _sigmoid(x):
    return 1.0 / (1.0 + jnp.exp(-x))


def _row(x, r):
    return jnp.broadcast_to(x[r:r + 1, :], (8, x.shape[1]))


def _decay_products(f):
    w = f.shape[1]
    r = lax.broadcasted_iota(jnp.int32, (8, w), 0)
    b0, b1, b2 = (r & 1) == 1, (r & 2) == 2, (r & 4) == 4
    lo = r < 4
    one = jnp.ones((8, w), F32)
    groups = [f[8 * j:8 * j + 8, :] for j in range(8)]
    p = {1: groups, 2: [], 4: [], 8: []}
    rr = {1: [one] * 8, 2: [], 4: [], 8: []}
    for x1 in groups:
        x2 = x1 * jnp.where(b0, pltpu.roll(x1, 1, 0), one)
        x4 = x2 * jnp.where(b1, jnp.where(lo, _row(x2, 1), _row(x2, 5)), one)
        x8 = x4 * jnp.where(b2, _row(x4, 3), one)
        r2 = jnp.where(b0, one, pltpu.roll(x1, 7, 0))
        r4 = r2 * jnp.where(b1, one, jnp.where(lo, _row(x2, 3), _row(x2, 7)))
        r8 = r4 * jnp.where(b2, one, _row(x4, 7))
        p[2].append(x2); p[4].append(x4); p[8].append(x8)
        rr[2].append(r2); rr[4].append(r4); rr[8].append(r8)
    for m in (8, 16, 32):
        g = m // 8
        tot = [_row(p[m][g * b + g - 1], 7) for b in range(64 // m)]
        pn, rn = [], []
        for j in range(8):
            b = j // g
            pn.append(p[m][j] * tot[b - 1] if b % 2 == 1 else p[m][j])
            rn.append(rr[m][j] * tot[b + 1] if b % 2 == 0 else rr[m][j])
        p[2 * m], rr[2 * m] = pn, rn
    cat = lambda xs: jnp.concatenate(xs, axis=0)
    return {m: (cat(p[m]), cat(rr[m])) for m in (1, 2, 4, 8, 16, 32, 64)}


def _hgrn_kernel(q_ref, f_ref, i_ref, g_ref, lb_ref, gain_ref, mask_ref, y_ref, st_ref):
    @pl.when(pl.program_id(1) == 0)
    def _():
        st_ref[...] = jnp.zeros_like(st_ref)

    lb = lb_ref[...]
    oml = 1.0 - lb
    gain = gain_ref[...]
    nt = (((1,), (1,)), ((), ()))
    tn = (((0,), (0,)), ((), ()))

    def chunk(c, carry):
        r0 = pl.multiple_of(c * CHUNK, CHUNK)

        def load(ref):
            return jnp.concatenate(
                [ref[h, pl.ds(r0, CHUNK), :] for h in range(A_HEADS)], axis=-1).astype(F32)

        z, qq, vv, gg = load(f_ref), load(q_ref), load(i_ref), load(g_ref)
        e = jnp.exp(-jnp.abs(z))
        rcp = 1.0 / (1.0 + e)
        er = e * rcp
        pos = z >= 0.0
        f = lb + oml * jnp.where(pos, rcp, er)
        kk = oml * jnp.where(pos, er, rcp)
        qs = qq * _sigmoid(qq) * (A_D ** -0.5)
        gate = gg * _sigmoid(gg)
        prod = _decay_products(f)
        qlev = [(qs * prod[m][0]).astype(BF16) for m in A_LEVELS]
        klev = [(kk * prod[m][1]).astype(BF16) for m in A_LEVELS]
        q_in = (qs * prod[64][0]).astype(BF16)
        k_out = (kk * prod[64][1]).astype(BF16)
        ftot = prod[64][0][CHUNK - 1:CHUNK, :]
        qk = qs * kk
        vb = vv.astype(BF16)
        for h in range(A_HEADS):
            sl = slice(h * A_D, (h + 1) * A_D)
            s_tot = jnp.zeros((CHUNK, CHUNK), F32)
            for li in range(len(A_LEVELS)):
                s = lax.dot_general(qlev[li][:, sl], klev[li][:, sl], nt,
                                    preferred_element_type=F32)
                s_tot = s_tot + s * mask_ref[li]
            st = st_ref[h]
            o = jnp.dot(s_tot.astype(BF16), vb[:, sl], preferred_element_type=F32)
            o = o + lax.dot_general(q_in[:, sl], st.astype(BF16), nt,
                                    preferred_element_type=F32)
            o = o + jnp.sum(qk[:, sl], axis=-1, keepdims=True) * vv[:, sl]
            st_ref[h] = st * ftot[:, sl] + lax.dot_general(
                vb[:, sl], k_out[:, sl], tn, preferred_element_type=F32)
            o = o * lax.rsqrt(jnp.mean(o * o, axis=-1, keepdims=True) + EPS) * gain
            y_ref[pl.ds(r0, CHUNK), sl] = (o * gate[:, sl]).astype(BF16)
        return carry

    lax.fori_loop(0, TA // CHUNK, chunk, 0, unroll=2)


def _hgrn_masks():
    t = np.arange(CHUNK)[:, None]
    s = np.arange(CHUNK)[None, :]
    out = []
    for m in A_LEVELS:
        ok = (t // (2 * m) == s // (2 * m)) & ((t // m) % 2 == 1) & ((s // m) % 2 == 0)
        out.append(ok)
    return np.stack(out).astype(np.float32)


def _hgrn(proj, lb, gain, bsz, seq):
    m = bsz * seq
    nb = seq // TA
    spec = lambda blk: pl.BlockSpec((A_HEADS, TA, LANE), lambda b, i, blk=blk: (blk, b * nb + i, 0))
    return pl.pallas_call(
        _hgrn_kernel,
        grid=(bsz, nb),
        in_specs=[spec(SLOT_AQ), spec(SLOT_AF), spec(SLOT_AI), spec(SLOT_AG),
                  pl.BlockSpec((1, A_WIDTH), lambda b, i: (0, 0)),
                  pl.BlockSpec((1, A_D), lambda b, i: (0, 0)),
                  pl.BlockSpec((len(A_LEVELS), CHUNK, CHUNK), lambda b, i: (0, 0, 0))],
        out_specs=pl.BlockSpec((TA, A_WIDTH), lambda b, i: (b * nb + i, 0)),
        out_shape=jax.ShapeDtypeStruct((m, A_WIDTH), BF16),
        scratch_shapes=[pltpu.VMEM((A_HEADS, A_D, A_D), F32)],
        compiler_params=_cparams(("arbitrary", "arbitrary")),
        name="hgrn",
    )(proj, proj, proj, proj, lb.reshape(1, A_WIDTH), gain.reshape(1, A_D),
      jnp.asarray(_hgrn_masks()))


def _band_kernel(q_ref, k0_ref, k1_ref, k2_ref, v0_ref, v1_ref, v2_ref, g_ref, bias_ref, y_ref):
    nt = (((1,), (1,)), ((), ()))
    k_refs = (k0_ref, k1_ref, k2_ref)
    v_refs = (v0_ref, v1_ref, v2_ref)
    ones = jnp.ones((3 * TB, B_DH), BF16)
    for h in range(B_HEADS):
        q = q_ref[h]
        s = [lax.dot_general(q, k_refs[j][h], nt, preferred_element_type=F32)
             + bias_ref[0, h, :, j * TB:(j + 1) * TB] for j in range(3)]
        mx = jnp.max(jnp.maximum(jnp.maximum(s[0], s[1]), s[2]), axis=-1, keepdims=True)
        p = jnp.concatenate([jnp.exp2(sj - mx).astype(BF16) for sj in s], axis=1)
        v = jnp.concatenate([v_refs[0][h], v_refs[1][h], v_refs[2][h]], axis=0)
        o = jnp.dot(p, jnp.concatenate([v, ones], axis=1), preferred_element_type=F32)
        g = g_ref[h].astype(F32)
        y_ref[:, h * B_DH:(h + 1) * B_DH] = (
            o[:, :B_DH] / o[:, B_DH:] * (g * _sigmoid(g))).astype(BF16)


def _toeplitz(u):
    hh = u.shape[0]
    ext = jnp.concatenate([u[:, ::-1], jnp.zeros((hh, 1), u.dtype)], axis=1)
    win = jnp.tile(ext, (1, CHUNK + 2))[:, :CHUNK * (LANE + 1)]
    win = win.reshape(hh, CHUNK, LANE + 1)[:, :, :CHUNK]
    return win[:, ::-1, :]


def _band_bias(rel_bias):
    rb = rel_bias.astype(F32) * LOG2E
    hh = rb.shape[0]
    last = rb[:, 2 * REL_CLIP:]
    far = jnp.broadcast_to(last[:, :, None], (hh, CHUNK, CHUNK))
    tile = {
        B_PREV: _toeplitz(rb[:, REL_CLIP - 63:REL_CLIP + 64]),
        B_PREV - 1: _toeplitz(rb[:, REL_CLIP + 1:2 * REL_CLIP]),
        B_PREV - 2: _toeplitz(jnp.concatenate(
            [rb[:, REL_CLIP + 65:], jnp.broadcast_to(last, (hh, CHUNK - 1))], axis=1)),
    }
    neg = jnp.full((hh, CHUNK, CHUNK), NEG_INF, F32)
    rows = []
    for cq in range(TB // CHUNK):
        cols = []
        for w in range(3 * TB // CHUNK):
            jj = w - cq
            cols.append(neg if (jj < 0 or jj > B_PREV) else tile.get(jj, far))
        rows.append(jnp.concatenate(cols, axis=2))
    full = jnp.concatenate(rows, axis=1)
    col = np.arange(3 * TB)[None, None, :]
    step0 = jnp.where(col < 2 * TB, NEG_INF, full)
    step1 = jnp.where(col < TB, NEG_INF, full)
    return jnp.stack([step0, step1, full])


def _band(proj, bias, bsz, seq):
    m = bsz * seq
    nb = seq // TB
    cur = lambda blk: pl.BlockSpec((B_HEADS, TB, LANE), lambda b, i, blk=blk: (blk, b * nb + i, 0))
    prev = lambda blk, d: pl.BlockSpec(
        (B_HEADS, TB, LANE), lambda b, i, blk=blk, d=d: (blk, b * nb + jnp.maximum(i - d, 0), 0))
    return pl.pallas_call(
        _band_kernel,
        grid=(bsz, nb),
        in_specs=[cur(SLOT_BQ),
                  prev(SLOT_BK, 2), prev(SLOT_BK, 1), cur(SLOT_BK),
                  prev(SLOT_BV, 2), prev(SLOT_BV, 1), cur(SLOT_BV),
                  cur(SLOT_BG),
                  pl.BlockSpec((1, B_HEADS, TB, 3 * TB),
                               lambda b, i: (jnp.minimum(i, 2), 0, 0, 0))],
        out_specs=pl.BlockSpec((TB, B_WIDTH), lambda b, i: (b * nb + i, 0)),
        out_shape=jax.ShapeDtypeStruct((m, B_WIDTH), BF16),
        compiler_params=_cparams(("arbitrary", "arbitrary")),
        name="band",
    )(proj, proj, proj, proj, proj, proj, proj, proj, bias)


def _ret_kernel(qk_ref, v_ref, g_ref, cos_ref, sin_ref, dmask_ref, qdec_ref, kdec_ref,
                kin_ref, cdec_ref, y_ref, st_ref):
    @pl.when(pl.program_id(1) == 0)
    def _():
        st_ref[...] = jnp.zeros_like(st_ref)

    nt = (((1,), (1,)), ((), ()))
    tn = (((0,), (0,)), ((), ()))
    cos = cos_ref[...]
    sin = sin_ref[...]
    lane = lax.broadcasted_iota(jnp.int32, (TC, LANE), 1)
    first_half = (lane % C_DK) < (C_DK // 2)

    def rotary(x):
        swapped = jnp.where(first_half, pltpu.roll(x, LANE - C_DK // 2, 1),
                            pltpu.roll(x, C_DK // 2, 1))
        return x * cos + swapped * sin

    rot = [rotary(qk_ref[t].astype(F32)) for t in range(C_QK_TILES)]
    qrot = rot[:3]
    krot = [pltpu.roll(x, C_DK, 1) for x in rot[2:]]
    for h in range(C_HEADS):
        qp, kp = qrot[h // 2], krot[(h + 1) // 2]
        v = v_ref[h]
        s = lax.dot_general(qp.astype(BF16), (kp * kin_ref[h]).astype(BF16), nt,
                            preferred_element_type=F32)
        s = s * dmask_ref[h]
        st = st_ref[h]
        o = jnp.dot(s.astype(BF16), v, preferred_element_type=F32)
        o = o + jnp.dot((qp * qdec_ref[h]).astype(BF16), st.astype(BF16),
                        preferred_element_type=F32)
        st_ref[h] = st * cdec_ref[h] + lax.dot_general(
            (kp * kdec_ref[h]).astype(BF16), v, tn, preferred_element_type=F32)
        o = o * lax.rsqrt(jnp.mean(o * o, axis=-1, keepdims=True) + EPS)
        g = g_ref[h].astype(F32)
        y_ref[:, h * C_DV:(h + 1) * C_DV] = (o * (g * _sigmoid(g))).astype(BF16)


def _ret_consts():
    hh = np.arange(C_HEADS, dtype=np.float64)
    log_gamma = np.log1p(-np.exp2(-5.0 - hh))
    pos = np.arange(TC, dtype=np.float64)
    rel = pos[:, None] - pos[None, :]
    dmask = np.where(rel >= 0, np.exp(log_gamma[:, None, None] * np.maximum(rel, 0.0)), 0.0)
    lane = np.arange(LANE)
    head_lanes = np.stack([(lane // C_DK) == (h % 2) for h in range(C_HEADS)]).astype(np.float64)
    qdec = np.exp(log_gamma[:, None] * (pos + 1.0)[None, :])[:, :, None] * head_lanes[:, None, :]
    kdec = (np.exp(log_gamma[:, None] * (TC - 1 - pos)[None, :])[:, :, None]
            * head_lanes[:, None, :] * C_DK ** -0.5)
    kin = head_lanes[:, None, :] * C_DK ** -0.5
    cdec = np.broadcast_to(np.exp(log_gamma * TC)[:, None, None], (C_HEADS, 1, C_DV))
    f = lambda a: jnp.asarray(np.ascontiguousarray(a), dtype=F32)
    return f(dmask), f(qdec), f(kdec), f(kin), f(cdec)


def _rope_tables(seq):
    inv_freq = ROPE_BASE ** (-jnp.linspace(0.0, 1.0, C_DK // 2, dtype=F32))
    ang = jnp.arange(seq, dtype=F32)[:, None] * inv_freq[None, :]
    cos, sin = jnp.cos(ang), jnp.sin(ang)
    cos_t = jnp.tile(cos, (1, LANE // (C_DK // 2)))
    sin_t = jnp.tile(jnp.concatenate([-sin, sin], axis=1), (1, LANE // C_DK))
    return cos_t, sin_t


def _ret(proj, cos_t, sin_t, consts, bsz, seq):
    m = bsz * seq
    nb = seq // TC
    dmask, qdec, kdec, kin, cdec = consts
    spec = lambda blk: pl.BlockSpec((C_HEADS, TC, LANE), lambda b, i, blk=blk: (blk, b * nb + i, 0))
    const = lambda shape: pl.BlockSpec(shape, lambda b, i: (0,) * len(shape))
    return pl.pallas_call(
        _ret_kernel,
        grid=(bsz, nb),
        in_specs=[spec(SLOT_CQK), spec(SLOT_CV), spec(SLOT_CG),
                  pl.BlockSpec((TC, LANE), lambda b, i: (i, 0)),
                  pl.BlockSpec((TC, LANE), lambda b, i: (i, 0)),
                  const(dmask.shape), const(qdec.shape), const(kdec.shape),
                  const(kin.shape), const(cdec.shape)],
        out_specs=pl.BlockSpec((TC, C_WIDTH), lambda b, i: (b * nb + i, 0)),
        out_shape=jax.ShapeDtypeStruct((m, C_WIDTH), BF16),
        scratch_shapes=[pltpu.VMEM((C_HEADS, LANE, C_DV), F32)],
        compiler_params=_cparams(("arbitrary", "arbitrary")),
        name="retention",
    )(proj, proj, proj, cos_t, sin_t, dmask, qdec, kdec, kin, cdec)


def _outproj_kernel(ya_ref, yb_ref, yc_ref, x_ref, w_ref, g_ref, *out_refs, last):
    y = jnp.concatenate([ya_ref[...], yb_ref[...], yc_ref[...]], axis=-1)
    xn = x_ref[...] + jnp.dot(y, w_ref[...], preferred_element_type=F32)
    ms = jnp.mean(xn * xn, axis=-1, keepdims=True)
    hn = xn * lax.rsqrt(ms + EPS) * g_ref[...]
    if last:
        out_refs[0][...] = hn
    else:
        out_refs[0][...] = xn
        out_refs[1][...] = hn.astype(BF16)


def _outproj(ya, yb, yc, x2, w, gain, last):
    m = x2.shape[0]
    row = lambda width: pl.BlockSpec((TM_OUT, width), lambda i: (i, 0))
    if last:
        out_specs = row(D_MODEL)
        out_shape = jax.ShapeDtypeStruct((m, D_MODEL), F32)
    else:
        out_specs = [row(D_MODEL), row(D_MODEL)]
        out_shape = [jax.ShapeDtypeStruct((m, D_MODEL), F32),
                     jax.ShapeDtypeStruct((m, D_MODEL), BF16)]
    return pl.pallas_call(
        functools.partial(_outproj_kernel, last=last),
        grid=(m // TM_OUT,),
        in_specs=[row(A_WIDTH), row(B_WIDTH), row(C_WIDTH), row(D_MODEL),
                  pl.BlockSpec((D_MIX, D_MODEL), lambda i: (0, 0)),
                  pl.BlockSpec((1, D_MODEL), lambda i: (0, 0))],
        out_specs=out_specs,
        out_shape=out_shape,
        compiler_params=_cparams(("arbitrary",)),
        name="outproj_last" if last else "outproj",
    )(ya, yb, yc, x2, w, gain.reshape(1, D_MODEL))


def _prep_w_in(w):
    a_end = 4 * A_WIDTH
    bq = w[:, a_end:a_end + B_WIDTH] * (B_DH ** -0.5 * LOG2E)
    pad = jnp.zeros((w.shape[0], LANE), w.dtype)
    return jnp.concatenate([w[:, :a_end], pad, bq, w[:, a_end + B_WIDTH:]], axis=1).astype(BF16)


def kernel(x, w_in, norm_gain, lb_logits, hgrn_norm_gain, rel_bias, w_out, final_gain):
    bsz, seq, _ = x.shape
    m = bsz * seq
    lb_all = jnp.cumsum(jax.nn.softmax(lb_logits.astype(F32), axis=0), axis=0)
    lb_all = lb_all - lb_all[0:1]
    cos_t, sin_t = _rope_tables(seq)
    ret_consts = _ret_consts()
    x2 = x.reshape(m, D_MODEL)
    h = None
    for layer in range(DEPTH):
        w = _prep_w_in(w_in[layer])
        proj = _inproj(x2, w, norm_gain[0]) if layer == 0 else _inproj(h, w)
        ya = _hgrn(proj, lb_all[layer], hgrn_norm_gain[layer], bsz, seq)
        yb = _band(proj, _band_bias(rel_bias[layer]), bsz, seq)
        yc = _ret(proj, cos_t, sin_t, ret_consts, bsz, seq)
        last = layer == DEPTH - 1
        gain = final_gain if last else norm_gain[layer + 1]
        res = _outproj(ya, yb, yc, x2, w_out[layer].astype(BF16), gain, last)
        if last:
            x2 = res
        else:
            x2, h = res
    return x2.reshape(bsz, seq, D_MODEL)
```

```python
import functools

import numpy as np
import jax
import jax.numpy as jnp
from jax import lax
from jax.experimental import pallas as pl
from jax.experimental.pallas import tpu as pltpu

F32 = jnp.float32
BF16 = jnp.bfloat16

D_MODEL = 2048
DEPTH = 2
CHUNK = 64
EPS = 1e-6
NEG_INF = -1e30
LOG2E = 1.4426950408889634
LANE = 128

A_HEADS, A_D = 6, 128
A_WIDTH = A_HEADS * A_D
B_HEADS, B_DH = 5, 128
B_WIDTH = B_HEADS * B_DH
B_PREV = 8
REL_CLIP = 128
C_HEADS, C_DK, C_DV = 5, 64, 128
C_QK = C_HEADS * C_DK
C_WIDTH = C_HEADS * C_DV
ROPE_BASE = 10000.0
D_MIX = A_WIDTH + B_WIDTH + C_WIDTH
D_IN_PAD = 7680
N_SLOTS = D_IN_PAD // LANE

SLOT_AQ, SLOT_AF, SLOT_AI, SLOT_AG = 0, 1, 2, 3
SLOT_BQ, SLOT_BK, SLOT_BV, SLOT_BG = 5, 6, 7, 8
SLOT_CQK, SLOT_CV, SLOT_CG = 9, 10, 11
C_QK_TILES = 2 * C_QK // LANE

TM_PROJ = 1024
TN_PROJ = 1536
TN_SUB = 512
TM_OUT = 512
TA = 1024
TB = 256
TC = 256
A_LEVELS = (1, 2, 4, 8, 16, 32)
VMEM_LIMIT = 56 * 1024 * 1024


def _cparams(sem):
    return pltpu.CompilerParams(dimension_semantics=sem, vmem_limit_bytes=VMEM_LIMIT)


def _project(h, w_ref, o_ref):
    for c in range(0, TN_PROJ, TN_SUB):
        acc = jnp.dot(h, w_ref[:, c:c + TN_SUB], preferred_element_type=F32)
        for k in range(TN_SUB // LANE):
            o_ref[c // LANE + k] = acc[:, k * LANE:(k + 1) * LANE].astype(BF16)


def _inproj_kernel(h_ref, w_ref, o_ref):
    _project(h_ref[...], w_ref, o_ref)


def _inproj_norm_kernel(x_ref, g_ref, w_ref, o_ref, h_ref):
    @pl.when(pl.program_id(1) == 0)
    def _():
        x = x_ref[...]
        ms = jnp.mean(x * x, axis=-1, keepdims=True)
        h_ref[...] = (x * lax.rsqrt(ms + EPS) * g_ref[...]).astype(BF16)

    _project(h_ref[...], w_ref, o_ref)


def _inproj(h, w, gain=None):
    m = h.shape[0]
    spb = TN_PROJ // LANE
    rows = pl.BlockSpec((TM_PROJ, D_MODEL), lambda i, j: (i, 0))
    wspec = pl.BlockSpec((pl.Element(D_MODEL), pl.Element(TN_PROJ)),
                         lambda i, j: (0, (j * (TN_PROJ // LANE) - jnp.where(j >= 2, 1, 0)) * LANE))
    common = dict(
        grid=(m // TM_PROJ, D_IN_PAD // TN_PROJ),
        out_specs=pl.BlockSpec((spb, TM_PROJ, LANE), lambda i, j: (j, i, 0)),
        out_shape=jax.ShapeDtypeStruct((N_SLOTS, m, LANE), BF16),
        compiler_params=_cparams(("arbitrary", "arbitrary")),
    )
    if gain is None:
        return pl.pallas_call(_inproj_kernel, in_specs=[rows, wspec], name="inproj",
                              **common)(h, w)
    return pl.pallas_call(
        _inproj_norm_kernel,
        in_specs=[rows, pl.BlockSpec((1, D_MODEL), lambda i, j: (0, 0)), wspec],
        scratch_shapes=[pltpu.VMEM((TM_PROJ, D_MODEL), BF16)],
        name="inproj_norm", **common)(h, gain.reshape(1, D_MODEL), w)


def _sigmoid(x):
    return 1.0 / (1.0 + jnp.exp(-x))


def _row(x, r):
    return jnp.broadcast_to(x[r:r + 1, :], (8, x.shape[1]))


def _decay_products(f):
    w = f.shape[1]
    r = lax.broadcasted_iota(jnp.int32, (8, w), 0)
    b0, b1, b2 = (r & 1) == 1, (r & 2) == 2, (r & 4) == 4
    lo = r < 4
    one = jnp.ones((8, w), F32)
    groups = [f[8 * j:8 * j + 8, :] for j in range(8)]
    p = {1: groups, 2: [], 4: [], 8: []}
    rr = {1: [one] * 8, 2: [], 4: [], 8: []}
    for x1 in groups:
        x2 = x1 * jnp.where(b0, pltpu.roll(x1, 1, 0), one)
        x4 = x2 * jnp.where(b1, jnp.where(lo, _row(x2, 1), _row(x2, 5)), one)
        x8 = x4 * jnp.where(b2, _row(x4, 3), one)
        r2 = jnp.where(b0, one, pltpu.roll(x1, 7, 0))
        r4 = r2 * jnp.where(b1, one, jnp.where(lo, _row(x2, 3), _row(x2, 7)))
        r8 = r4 * jnp.where(b2, one, _row(x4, 7))
        p[2].append(x2); p[4].append(x4); p[8].append(x8)
        rr[2].append(r2); rr[4].append(r4); rr[8].append(r8)
    for m in (8, 16, 32):
        g = m // 8
        tot = [_row(p[m][g * b + g - 1], 7) for b in range(64 // m)]
        pn, rn = [], []
        for j in range(8):
            b = j // g
            pn.append(p[m][j] * tot[b - 1] if b % 2 == 1 else p[m][j])
            rn.append(rr[m][j] * tot[b + 1] if b % 2 == 0 else rr[m][j])
        p[2 * m], rr[2 * m] = pn, rn
    cat = lambda xs: jnp.concatenate(xs, axis=0)
    return {m: (cat(p[m]), cat(rr[m])) for m in (1, 2, 4, 8, 16, 32, 64)}


N_LEV = len(A_LEVELS)


def _hgrn_kernel(q_ref, f_ref, i_ref, g_ref, lb_ref, gain_ref, mask_ref, y_ref, st_ref,
                 lev0, aux0, dec0, lev1, aux1, dec1):
    @pl.when(pl.program_id(1) == 0)
    def _():
        st_ref[...] = jnp.zeros_like(st_ref)

    oml = 1.0 - lb_ref[...]
    gain = gain_ref[...]
    nt = (((1,), (1,)), ((), ()))
    tn = (((0,), (0,)), ((), ()))
    bufs = ((lev0, aux0, dec0), (lev1, aux1, dec1))

    def prepare(c, buf):
        lev, aux, dec = buf
        r0 = pl.multiple_of(c * CHUNK, CHUNK)

        def load(ref):
            return jnp.concatenate(
                [ref[h, pl.ds(r0, CHUNK), :] for h in range(A_HEADS)], axis=-1).astype(F32)

        z, qq, vv, gg = load(f_ref), load(q_ref), load(i_ref), load(g_ref)
        e = jnp.exp(-jnp.abs(z))
        rcp = 1.0 / (1.0 + e)
        kk = oml * jnp.where(z >= 0.0, e * rcp, rcp)
        f = 1.0 - kk
        qs = qq * _sigmoid(qq) * (A_D ** -0.5)
        prod = _decay_products(f)
        qb, kb = qs.astype(BF16), kk.astype(BF16)
        for li, m in enumerate(A_LEVELS):
            lev[li] = qb * prod[m][0].astype(BF16)
            lev[N_LEV + li] = kb if m == 1 else kb * prod[m][1].astype(BF16)
        lev[2 * N_LEV] = qb * prod[64][0].astype(BF16)
        lev[2 * N_LEV + 1] = kb * prod[64][1].astype(BF16)
        dec[0:1, :] = prod[64][0][CHUNK - 1:CHUNK, :]
        qk = qs * kk
        aux[0] = vv
        aux[1] = jnp.concatenate(
            [jnp.sum(qk[:, h * A_D:(h + 1) * A_D], axis=-1, keepdims=True)
             * vv[:, h * A_D:(h + 1) * A_D] for h in range(A_HEADS)], axis=-1)
        aux[2] = gg * _sigmoid(gg)

    def mix(c, buf):
        lev, aux, dec = buf
        r0 = pl.multiple_of(c * CHUNK, CHUNK)
        heads = range(A_HEADS)
        sl = [slice(h * A_D, (h + 1) * A_D) for h in heads]
        s_tot = []
        for h in heads:
            rows = [jnp.zeros((8, CHUNK), F32)] * 8
            for li, m in enumerate(A_LEVELS):
                s = lax.dot_general(lev[li, :, sl[h]], lev[N_LEV + li, :, sl[h]], nt,
                                    preferred_element_type=F32)
                for j in range(8):
                    if m < 8 or (8 * j // m) % 2 == 1:
                        rows[j] = rows[j] + s[8 * j:8 * j + 8, :] * mask_ref[li, 8 * j:8 * j + 8, :]
            s_tot.append(jnp.concatenate(rows, axis=0).astype(BF16))
        st = [st_ref[h] for h in heads]
        vb = [aux[0, :, sl[h]].astype(BF16) for h in heads]
        o_in = [lax.dot_general(lev[2 * N_LEV, :, sl[h]], st[h].astype(BF16), nt,
                                preferred_element_type=F32) for h in heads]
        for h in heads:
            st_ref[h] = st[h] * dec[0:1, sl[h]] + lax.dot_general(
                vb[h], lev[2 * N_LEV + 1, :, sl[h]], tn, preferred_element_type=F32)
        for h in heads:
            o = jnp.dot(s_tot[h], vb[h], preferred_element_type=F32) + o_in[h] + aux[1, :, sl[h]]
            o = o * lax.rsqrt(jnp.mean(o * o, axis=-1, keepdims=True) + EPS) * gain
            y_ref[pl.ds(r0, CHUNK), sl[h]] = (o * aux[2, :, sl[h]]).astype(BF16)

    n_chunks = TA // CHUNK
    prepare(0, bufs[0])

    def pair(k, carry):
        c = 2 * k
        prepare(c + 1, bufs[1])
        mix(c, bufs[0])
        prepare(c + 2, bufs[0])
        mix(c + 1, bufs[1])
        return carry

    lax.fori_loop(0, n_chunks // 2 - 1, pair, 0)
    prepare(n_chunks - 1, bufs[1])
    mix(n_chunks - 2, bufs[0])
    mix(n_chunks - 1, bufs[1])


def _hgrn_masks():
    t = np.arange(CHUNK)[:, None]
    s = np.arange(CHUNK)[None, :]
    out = []
    for m in A_LEVELS:
        ok = (t // (2 * m) == s // (2 * m)) & ((t // m) % 2 == 1) & ((s // m) % 2 == 0)
        out.append(ok)
    return np.stack(out).astype(np.float32)


def _hgrn(proj, lb, gain, bsz, seq):
    m = bsz * seq
    nb = seq // TA
    spec = lambda blk: pl.BlockSpec((A_HEADS, TA, LANE), lambda b, i, blk=blk: (blk, b * nb + i, 0))
    return pl.pallas_call(
        _hgrn_kernel,
        grid=(bsz, nb),
        in_specs=[spec(SLOT_AQ), spec(SLOT_AF), spec(SLOT_AI), spec(SLOT_AG),
                  pl.BlockSpec((1, A_WIDTH), lambda b, i: (0, 0)),
                  pl.BlockSpec((1, A_D), lambda b, i: (0, 0)),
                  pl.BlockSpec((len(A_LEVELS), CHUNK, CHUNK), lambda b, i: (0, 0, 0))],
        out_specs=pl.BlockSpec((TA, A_WIDTH), lambda b, i: (b * nb + i, 0)),
        out_shape=jax.ShapeDtypeStruct((m, A_WIDTH), BF16),
        scratch_shapes=[pltpu.VMEM((A_HEADS, A_D, A_D), F32)] + 2 * [
            pltpu.VMEM((2 * N_LEV + 2, CHUNK, A_WIDTH), BF16),
            pltpu.VMEM((3, CHUNK, A_WIDTH), F32),
            pltpu.VMEM((8, A_WIDTH), F32)],
        compiler_params=_cparams(("arbitrary", "arbitrary")),
        name="hgrn",
    )(proj, proj, proj, proj, lb.reshape(1, A_WIDTH), gain.reshape(1, A_D),
      jnp.asarray(_hgrn_masks()))


def _band_kernel(q_ref, k0_ref, k1_ref, k2_ref, v0_ref, v1_ref, v2_ref, g_ref, bias_ref, y_ref):
    nt = (((1,), (1,)), ((), ()))
    k_refs = (k0_ref, k1_ref, k2_ref)
    v_refs = (v0_ref, v1_ref, v2_ref)
    ones = jnp.ones((3 * TB, B_DH), BF16)

    def scores(h):
        q = q_ref[h]
        return [lax.dot_general(q, k_refs[j][h], nt, preferred_element_type=F32)
                + bias_ref[0, h, :, j * TB:(j + 1) * TB] for j in range(3)]

    def attend(h, s):
        mx = jnp.max(jnp.maximum(jnp.maximum(s[0], s[1]), s[2]), axis=-1, keepdims=True)
        p = jnp.concatenate([jnp.exp2(sj - mx).astype(BF16) for sj in s], axis=1)
        v = jnp.concatenate([v_refs[0][h], v_refs[1][h], v_refs[2][h]], axis=0)
        o = jnp.dot(p, jnp.concatenate([v, ones], axis=1), preferred_element_type=F32)
        g = g_ref[h].astype(F32)
        y_ref[:, h * B_DH:(h + 1) * B_DH] = (
            o[:, :B_DH] / o[:, B_DH:] * (g * _sigmoid(g))).astype(BF16)

    s_next = scores(0)
    for h in range(B_HEADS):
        s_cur = s_next
        if h + 1 < B_HEADS:
            s_next = scores(h + 1)
        attend(h, s_cur)


def _toeplitz(u):
    hh = u.shape[0]
    ext = jnp.concatenate([u[:, ::-1], jnp.zeros((hh, 1), u.dtype)], axis=1)
    win = jnp.tile(ext, (1, CHUNK + 2))[:, :CHUNK * (LANE + 1)]
    win = win.reshape(hh, CHUNK, LANE + 1)[:, :, :CHUNK]
    return win[:, ::-1, :]


def _band_bias(rel_bias):
    rb = rel_bias.astype(F32) * LOG2E
    hh = rb.shape[0]
    last = rb[:, 2 * REL_CLIP:]
    far = jnp.broadcast_to(last[:, :, None], (hh, CHUNK, CHUNK))
    tile = {
        B_PREV: _toeplitz(rb[:, REL_CLIP - 63:REL_CLIP + 64]),
        B_PREV - 1: _toeplitz(rb[:, REL_CLIP + 1:2 * REL_CLIP]),
        B_PREV - 2: _toeplitz(jnp.concatenate(
            [rb[:, REL_CLIP + 65:], jnp.broadcast_to(last, (hh, CHUNK - 1))], axis=1)),
    }
    neg = jnp.full((hh, CHUNK, CHUNK), NEG_INF, F32)
    rows = []
    for cq in range(TB // CHUNK):
        cols = []
        for w in range(3 * TB // CHUNK):
            jj = w - cq
            cols.append(neg if (jj < 0 or jj > B_PREV) else tile.get(jj, far))
        rows.append(jnp.concatenate(cols, axis=2))
    full = jnp.concatenate(rows, axis=1)
    col = np.arange(3 * TB)[None, None, :]
    step0 = jnp.where(col < 2 * TB, NEG_INF, full)
    step1 = jnp.where(col < TB, NEG_INF, full)
    return jnp.stack([step0, step1, full])


def _band(proj, bias, bsz, seq):
    m = bsz * seq
    nb = seq // TB
    cur = lambda blk: pl.BlockSpec((B_HEADS, TB, LANE), lambda b, i, blk=blk: (blk, b * nb + i, 0))
    prev = lambda blk, d: pl.BlockSpec(
        (B_HEADS, TB, LANE), lambda b, i, blk=blk, d=d: (blk, b * nb + jnp.maximum(i - d, 0), 0))
    return pl.pallas_call(
        _band_kernel,
        grid=(bsz, nb),
        in_specs=[cur(SLOT_BQ),
                  prev(SLOT_BK, 2), prev(SLOT_BK, 1), cur(SLOT_BK),
                  prev(SLOT_BV, 2), prev(SLOT_BV, 1), cur(SLOT_BV),
                  cur(SLOT_BG),
                  pl.BlockSpec((1, B_HEADS, TB, 3 * TB),
                               lambda b, i: (jnp.minimum(i, 2), 0, 0, 0))],
        out_specs=pl.BlockSpec((TB, B_WIDTH), lambda b, i: (b * nb + i, 0)),
        out_shape=jax.ShapeDtypeStruct((m, B_WIDTH), BF16),
        compiler_params=_cparams(("arbitrary", "arbitrary")),
        name="band",
    )(proj, proj, proj, proj, proj, proj, proj, proj, bias)


def _ret_kernel(qk_ref, v_ref, g_ref, cos_ref, sin_ref, dmask_ref, qdec_ref, kdec_ref,
                kin_ref, cdec_ref, y_ref, st_ref):
    @pl.when(pl.program_id(1) == 0)
    def _():
        st_ref[...] = jnp.zeros_like(st_ref)

    nt = (((1,), (1,)), ((), ()))
    tn = (((0,), (0,)), ((), ()))
    cos = cos_ref[...]
    sin = sin_ref[...]
    lane = lax.broadcasted_iota(jnp.int32, (TC, LANE), 1)
    first_half = (lane % C_DK) < (C_DK // 2)

    def rotary(x):
        swapped = jnp.where(first_half, pltpu.roll(x, LANE - C_DK // 2, 1),
                            pltpu.roll(x, C_DK // 2, 1))
        return x * cos + swapped * sin

    rot = [rotary(qk_ref[t].astype(F32)) for t in range(C_QK_TILES)]
    qrot = rot[:3]
    krot = [pltpu.roll(x, C_DK, 1) for x in rot[2:]]
    heads = range(C_HEADS)
    qb = [x.astype(BF16) for x in qrot]
    s = [lax.dot_general(qb[h // 2], (krot[(h + 1) // 2] * kin_ref[h]).astype(BF16), nt,
                         preferred_element_type=F32) for h in heads]
    st = [st_ref[h] for h in heads]
    o_in = [jnp.dot((qrot[h // 2] * qdec_ref[h]).astype(BF16), st[h].astype(BF16),
                    preferred_element_type=F32) for h in heads]
    for h in heads:
        st_ref[h] = st[h] * cdec_ref[h] + lax.dot_general(
            (krot[(h + 1) // 2] * kdec_ref[h]).astype(BF16), v_ref[h], tn,
            preferred_element_type=F32)
    for h in heads:
        o = o_in[h] + jnp.dot((s[h] * dmask_ref[h]).astype(BF16), v_ref[h],
                              preferred_element_type=F32)
        o = o * lax.rsqrt(jnp.mean(o * o, axis=-1, keepdims=True) + EPS)
        g = g_ref[h].astype(F32)
        y_ref[:, h * C_DV:(h + 1) * C_DV] = (o * (g * _sigmoid(g))).astype(BF16)


def _ret_consts():
    hh = np.arange(C_HEADS, dtype=np.float64)
    log_gamma = np.log1p(-np.exp2(-5.0 - hh))
    pos = np.arange(TC, dtype=np.float64)
    rel = pos[:, None] - pos[None, :]
    dmask = np.where(rel >= 0, np.exp(log_gamma[:, None, None] * np.maximum(rel, 0.0)), 0.0)
    lane = np.arange(LANE)
    head_lanes = np.stack([(lane // C_DK) == (h % 2) for h in range(C_HEADS)]).astype(np.float64)
    qdec = np.exp(log_gamma[:, None] * (pos + 1.0)[None, :])[:, :, None] * head_lanes[:, None, :]
    kdec = (np.exp(log_gamma[:, None] * (TC - 1 - pos)[None, :])[:, :, None]
            * head_lanes[:, None, :] * C_DK ** -0.5)
    kin = head_lanes[:, None, :] * C_DK ** -0.5
    cdec = np.broadcast_to(np.exp(log_gamma * TC)[:, None, None], (C_HEADS, 1, C_DV))
    f = lambda a: jnp.asarray(np.ascontiguousarray(a), dtype=F32)
    return f(dmask), f(qdec), f(kdec), f(kin), f(cdec)


def _rope_tables(seq):
    inv_freq = ROPE_BASE ** (-jnp.linspace(0.0, 1.0, C_DK // 2, dtype=F32))
    ang = jnp.arange(seq, dtype=F32)[:, None] * inv_freq[None, :]
    cos, sin = jnp.cos(ang), jnp.sin(ang)
    cos_t = jnp.tile(cos, (1, LANE // (C_DK // 2)))
    sin_t = jnp.tile(jnp.concatenate([-sin, sin], axis=1), (1, LANE // C_DK))
    return cos_t, sin_t


def _ret(proj, cos_t, sin_t, consts, bsz, seq):
    m = bsz * seq
    nb = seq // TC
    dmask, qdec, kdec, kin, cdec = consts
    spec = lambda blk: pl.BlockSpec((C_HEADS, TC, LANE), lambda b, i, blk=blk: (blk, b * nb + i, 0))
    const = lambda shape: pl.BlockSpec(shape, lambda b, i: (0,) * len(shape))
    return pl.pallas_call(
        _ret_kernel,
        grid=(bsz, nb),
        in_specs=[spec(SLOT_CQK), spec(SLOT_CV), spec(SLOT_CG),
                  pl.BlockSpec((TC, LANE), lambda b, i: (i, 0)),
                  pl.BlockSpec((TC, LANE), lambda b, i: (i, 0)),
                  const(dmask.shape), const(qdec.shape), const(kdec.shape),
                  const(kin.shape), const(cdec.shape)],
        out_specs=pl.BlockSpec((TC, C_WIDTH), lambda b, i: (b * nb + i, 0)),
        out_shape=jax.ShapeDtypeStruct((m, C_WIDTH), BF16),
        scratch_shapes=[pltpu.VMEM((C_HEADS, LANE, C_DV), F32)],
        compiler_params=_cparams(("arbitrary", "arbitrary")),
        name="retention",
    )(proj, proj, proj, cos_t, sin_t, dmask, qdec, kdec, kin, cdec)


def _outproj_kernel(ya_ref, yb_ref, yc_ref, x_ref, w_ref, g_ref, *out_refs, last):
    y = jnp.concatenate([ya_ref[...], yb_ref[...], yc_ref[...]], axis=-1)
    xn = x_ref[...] + jnp.dot(y, w_ref[...], preferred_element_type=F32)
    ms = jnp.mean(xn * xn, axis=-1, keepdims=True)
    hn = xn * lax.rsqrt(ms + EPS) * g_ref[...]
    if last:
        out_refs[0][...] = hn
    else:
        out_refs[0][...] = xn
        out_refs[1][...] = hn.astype(BF16)


def _outproj(ya, yb, yc, x2, w, gain, last):
    m = x2.shape[0]
    row = lambda width: pl.BlockSpec((TM_OUT, width), lambda i: (i, 0))
    if last:
        out_specs = row(D_MODEL)
        out_shape = jax.ShapeDtypeStruct((m, D_MODEL), F32)
    else:
        out_specs = [row(D_MODEL), row(D_MODEL)]
        out_shape = [jax.ShapeDtypeStruct((m, D_MODEL), F32),
                     jax.ShapeDtypeStruct((m, D_MODEL), BF16)]
    return pl.pallas_call(
        functools.partial(_outproj_kernel, last=last),
        grid=(m // TM_OUT,),
        in_specs=[row(A_WIDTH), row(B_WIDTH), row(C_WIDTH), row(D_MODEL),
                  pl.BlockSpec((D_MIX, D_MODEL), lambda i: (0, 0)),
                  pl.BlockSpec((1, D_MODEL), lambda i: (0, 0))],
        out_specs=out_specs,
        out_shape=out_shape,
        compiler_params=_cparams(("arbitrary",)),
        name="outproj_last" if last else "outproj",
    )(ya, yb, yc, x2, w, gain.reshape(1, D_MODEL))


def _prep_w_in(w):
    scale = np.ones((1, w.shape[1]), np.float32)
    scale[:, 4 * A_WIDTH:4 * A_WIDTH + B_WIDTH] = B_DH ** -0.5 * LOG2E
    return (w * scale).astype(BF16)


def kernel(x, w_in, norm_gain, lb_logits, hgrn_norm_gain, rel_bias, w_out, final_gain):
    bsz, seq, _ = x.shape
    m = bsz * seq
    lb_all = jnp.cumsum(jax.nn.softmax(lb_logits.astype(F32), axis=0), axis=0)
    lb_all = lb_all - lb_all[0:1]
    cos_t, sin_t = _rope_tables(seq)
    ret_consts = _ret_consts()
    x2 = x.reshape(m, D_MODEL)
    h = None
    for layer in range(DEPTH):
        w = _prep_w_in(w_in[layer])
        proj = _inproj(x2, w, norm_gain[0]) if layer == 0 else _inproj(h, w)
        ya = _hgrn(proj, lb_all[layer], hgrn_norm_gain[layer], bsz, seq)
        yb = _band(proj, _band_bias(rel_bias[layer]), bsz, seq)
        yc = _ret(proj, cos_t, sin_t, ret_consts, bsz, seq)
        last = layer == DEPTH - 1
        gain = final_gain if last else norm_gain[layer + 1]
        res = _outproj(ya, yb, yc, x2, w_out[layer].astype(BF16), gain, last)
        if last:
            x2 = res
        else:
            x2, h = res
    return x2.reshape(bsz, seq, D_MODEL)
```

```python
import functools

import numpy as np
import jax
import jax.numpy as jnp
from jax import lax
from jax.experimental import pallas as pl
from jax.experimental.pallas import tpu as pltpu

F32 = jnp.float32
BF16 = jnp.bfloat16

D_MODEL = 2048
DEPTH = 2
CHUNK = 64
EPS = 1e-6
NEG_INF = -1e30
LOG2E = 1.4426950408889634
LANE = 128

A_HEADS, A_D = 6, 128
A_WIDTH = A_HEADS * A_D
B_HEADS, B_DH = 5, 128
B_WIDTH = B_HEADS * B_DH
B_PREV = 8
REL_CLIP = 128
C_HEADS, C_DK, C_DV = 5, 64, 128
C_QK = C_HEADS * C_DK
C_WIDTH = C_HEADS * C_DV
ROPE_BASE = 10000.0
D_MIX = A_WIDTH + B_WIDTH + C_WIDTH
D_IN_PAD = 7680
N_SLOTS = D_IN_PAD // LANE

SLOT_AQ, SLOT_AF, SLOT_AI, SLOT_AG = 0, 1, 2, 3
SLOT_BQ, SLOT_BK, SLOT_BV, SLOT_BG = 5, 6, 7, 8
SLOT_CQK, SLOT_CV, SLOT_CG = 9, 10, 11
C_QK_TILES = 2 * C_QK // LANE
SLOT_KIND = (["silu"] * A_HEADS + ["keep"] * A_HEADS + ["none"] * A_HEADS + ["silu"] * A_HEADS
             + ["none"] * (1 + 3 * B_HEADS) + ["silu"] * B_HEADS
             + ["none"] * (C_QK_TILES + C_HEADS) + ["silu"] * C_HEADS)

TM_PROJ = 1024
TN_PROJ = 1536
TN_SUB = 512
TM_OUT = 512
TM_OUT_SUB = 256
TA = 2048
TB = 256
TC = 256
A_LEVELS = (1, 2, 4, 8, 16, 32)
VMEM_LIMIT = 56 * 1024 * 1024


def _cparams(sem):
    return pltpu.CompilerParams(dimension_semantics=sem, vmem_limit_bytes=VMEM_LIMIT)


def _activate(t, slot, lb_ref):
    kind = SLOT_KIND[slot]
    if kind == "silu":
        half = 0.5 * t
        return half + half * jnp.tanh(half)
    if kind == "keep":
        head = slot - SLOT_AF * A_HEADS
        oml = 1.0 - lb_ref[:, head * A_D:(head + 1) * A_D]
        e = jnp.exp(-jnp.abs(t))
        rcp = 1.0 / (1.0 + e)
        return oml * jnp.where(t >= 0.0, e * rcp, rcp)
    return t


def _project(h_ref, w_ref, lb_ref, o_ref):
    spb = TN_PROJ // LANE
    for jj in range(D_IN_PAD // TN_PROJ):
        @pl.when(pl.program_id(1) == jj)
        def _(jj=jj):
            h = h_ref[...]
            cost = {"keep": 0, "silu": 1, "none": 2}
            order = lambda c: min(cost[SLOT_KIND[jj * spb + c // LANE + k]]
                                  for k in range(TN_SUB // LANE))
            for c in sorted(range(0, TN_PROJ, TN_SUB), key=order):
                acc = jnp.dot(h, w_ref[:, c:c + TN_SUB], preferred_element_type=F32)
                for k in range(TN_SUB // LANE):
                    t = _activate(acc[:, k * LANE:(k + 1) * LANE], jj * spb + c // LANE + k, lb_ref)
                    o_ref[c // LANE + k] = t.astype(BF16)


def _inproj_kernel(h_ref, w_ref, lb_ref, o_ref):
    _project(h_ref, w_ref, lb_ref, o_ref)


def _inproj_norm_kernel(x_ref, g_ref, w_ref, lb_ref, o_ref, h_ref):
    @pl.when(pl.program_id(1) == 0)
    def _():
        x = x_ref[...]
        ms = jnp.mean(x * x, axis=-1, keepdims=True)
        h_ref[...] = (x * lax.rsqrt(ms + EPS) * g_ref[...]).astype(BF16)

    _project(h_ref, w_ref, lb_ref, o_ref)


def _inproj(h, w, lb, gain=None):
    m = h.shape[0]
    spb = TN_PROJ // LANE
    rows = pl.BlockSpec((TM_PROJ, D_MODEL), lambda i, j: (i, 0))
    wspec = pl.BlockSpec((pl.Element(D_MODEL), pl.Element(TN_PROJ)),
                         lambda i, j: (0, (j * (TN_PROJ // LANE) - jnp.where(j >= 2, 1, 0)) * LANE))
    common = dict(
        grid=(m // TM_PROJ, D_IN_PAD // TN_PROJ),
        out_specs=pl.BlockSpec((spb, TM_PROJ, LANE), lambda i, j: (j, i, 0)),
        out_shape=jax.ShapeDtypeStruct((N_SLOTS, m, LANE), BF16),
        compiler_params=_cparams(("arbitrary", "arbitrary")),
    )
    lbspec = pl.BlockSpec((1, A_WIDTH), lambda i, j: (0, 0))
    lb = lb.reshape(1, A_WIDTH)
    if gain is None:
        return pl.pallas_call(_inproj_kernel, in_specs=[rows, wspec, lbspec], name="inproj",
                              **common)(h, w, lb)
    return pl.pallas_call(
        _inproj_norm_kernel,
        in_specs=[rows, pl.BlockSpec((1, D_MODEL), lambda i, j: (0, 0)), wspec, lbspec],
        scratch_shapes=[pltpu.VMEM((TM_PROJ, D_MODEL), BF16)],
        name="inproj_norm", **common)(h, gain.reshape(1, D_MODEL), w, lb)


def _row(x, r):
    return jnp.broadcast_to(x[r:r + 1, :], (8, x.shape[1]))


def _decay_products(f):
    w = f.shape[1]
    r = lax.broadcasted_iota(jnp.int32, (8, w), 0)
    b0, b1, b2 = (r & 1) == 1, (r & 2) == 2, (r & 4) == 4
    lo = r < 4
    one = jnp.ones((8, w), F32)
    groups = [f[8 * j:8 * j + 8, :] for j in range(8)]
    p = {1: groups, 2: [], 4: [], 8: []}
    rr = {1: [one] * 8, 2: [], 4: [], 8: []}
    for x1 in groups:
        x2 = x1 * jnp.where(b0, pltpu.roll(x1, 1, 0), one)
        x4 = x2 * jnp.where(b1, jnp.where(lo, _row(x2, 1), _row(x2, 5)), one)
        x8 = x4 * jnp.where(b2, _row(x4, 3), one)
        r2 = jnp.where(b0, one, pltpu.roll(x1, 7, 0))
        r4 = r2 * jnp.where(b1, one, jnp.where(lo, _row(x2, 3), _row(x2, 7)))
        r8 = r4 * jnp.where(b2, one, _row(x4, 7))
        p[2].append(x2); p[4].append(x4); p[8].append(x8)
        rr[2].append(r2); rr[4].append(r4); rr[8].append(r8)
    for m in (8, 16, 32):
        g = m // 8
        tot = [_row(p[m][g * b + g - 1], 7) for b in range(64 // m)]
        pn, rn = [], []
        for j in range(8):
            b = j // g
            pn.append(p[m][j] * tot[b - 1] if b % 2 == 1 else p[m][j])
            rn.append(rr[m][j] * tot[b + 1] if b % 2 == 0 else rr[m][j])
        p[2 * m], rr[2 * m] = pn, rn
    cat = lambda xs: jnp.concatenate(xs, axis=0)
    return {m: (cat(p[m]), cat(rr[m])) for m in (1, 2, 4, 8, 16, 32, 64)}


N_LEV = len(A_LEVELS)


def _hgrn_kernel(q_ref, k_ref, i_ref, g_ref, gain_ref, mask_ref, y_ref, st_ref,
                 lev0, aux0, dec0, lev1, aux1, dec1):
    @pl.when(pl.program_id(1) == 0)
    def _():
        st_ref[...] = jnp.zeros_like(st_ref)

    gain = gain_ref[...]
    nt = (((1,), (1,)), ((), ()))
    tn = (((0,), (0,)), ((), ()))
    bufs = ((lev0, aux0, dec0), (lev1, aux1, dec1))

    def prepare(c, buf):
        lev, aux, dec = buf
        r0 = pl.multiple_of(c * CHUNK, CHUNK)

        def load(ref):
            return jnp.concatenate(
                [ref[h, pl.ds(r0, CHUNK), :] for h in range(A_HEADS)], axis=-1)

        qb, kb = load(q_ref), load(k_ref)
        qs, kk, vv = qb.astype(F32), kb.astype(F32), load(i_ref).astype(F32)
        prod = _decay_products(1.0 - kk)
        for li, m in enumerate(A_LEVELS):
            lev[li] = qb * prod[m][0].astype(BF16)
            lev[N_LEV + li] = kb if m == 1 else kb * prod[m][1].astype(BF16)
        lev[2 * N_LEV] = qb * prod[64][0].astype(BF16)
        lev[2 * N_LEV + 1] = kb * prod[64][1].astype(BF16)
        dec[0:1, :] = prod[64][0][CHUNK - 1:CHUNK, :]
        qk = qs * kk
        aux[...] = jnp.concatenate(
            [jnp.sum(qk[:, h * A_D:(h + 1) * A_D], axis=-1, keepdims=True)
             * vv[:, h * A_D:(h + 1) * A_D] for h in range(A_HEADS)], axis=-1)

    def mix(c, buf):
        lev, aux, dec = buf
        r0 = pl.multiple_of(c * CHUNK, CHUNK)
        heads = range(A_HEADS)
        sl = [slice(h * A_D, (h + 1) * A_D) for h in heads]
        s_tot = []
        for h in heads:
            rows = [jnp.zeros((8, CHUNK), F32)] * 8
            for li, m in enumerate(A_LEVELS):
                s = lax.dot_general(lev[li, :, sl[h]], lev[N_LEV + li, :, sl[h]], nt,
                                    preferred_element_type=F32)
                for j in range(8):
                    if m < 8 or (8 * j // m) % 2 == 1:
                        rows[j] = rows[j] + s[8 * j:8 * j + 8, :] * mask_ref[li, 8 * j:8 * j + 8, :]
            s_tot.append(jnp.concatenate(rows, axis=0).astype(BF16))
        st = [st_ref[h] for h in heads]
        vb = [i_ref[h, pl.ds(r0, CHUNK), :] for h in heads]
        o_in = [lax.dot_general(lev[2 * N_LEV, :, sl[h]], st[h].astype(BF16), nt,
                                preferred_element_type=F32) for h in heads]
        for h in heads:
            st_ref[h] = st[h] * dec[0:1, sl[h]] + lax.dot_general(
                vb[h], lev[2 * N_LEV + 1, :, sl[h]], tn, preferred_element_type=F32)
        for h in heads:
            o = jnp.dot(s_tot[h], vb[h], preferred_element_type=F32) + o_in[h] + aux[:, sl[h]]
            o = o * lax.rsqrt(jnp.mean(o * o, axis=-1, keepdims=True) + EPS * A_D) * gain
            y_ref[pl.ds(r0, CHUNK), sl[h]] = o.astype(BF16) * g_ref[h, pl.ds(r0, CHUNK), :]

    n_chunks = TA // CHUNK
    prepare(0, bufs[0])

    def pair(k, carry):
        c = 2 * k
        prepare(c + 1, bufs[1])
        mix(c, bufs[0])
        prepare(c + 2, bufs[0])
        mix(c + 1, bufs[1])
        return carry

    lax.fori_loop(0, n_chunks // 2 - 1, pair, 0)
    prepare(n_chunks - 1, bufs[1])
    mix(n_chunks - 2, bufs[0])
    mix(n_chunks - 1, bufs[1])


def _hgrn_masks():
    t = np.arange(CHUNK)[:, None]
    s = np.arange(CHUNK)[None, :]
    out = []
    for m in A_LEVELS:
        ok = (t // (2 * m) == s // (2 * m)) & ((t // m) % 2 == 1) & ((s // m) % 2 == 0)
        out.append(ok)
    return np.stack(out).astype(np.float32)


def _hgrn(proj, gain, bsz, seq):
    m = bsz * seq
    nb = seq // TA
    spec = lambda blk: pl.BlockSpec((A_HEADS, TA, LANE), lambda b, i, blk=blk: (blk, b * nb + i, 0))
    return pl.pallas_call(
        _hgrn_kernel,
        grid=(bsz, nb),
        in_specs=[spec(SLOT_AQ), spec(SLOT_AF), spec(SLOT_AI), spec(SLOT_AG),
                  pl.BlockSpec((1, A_D), lambda b, i: (0, 0)),
                  pl.BlockSpec((len(A_LEVELS), CHUNK, CHUNK), lambda b, i: (0, 0, 0))],
        out_specs=pl.BlockSpec((TA, A_WIDTH), lambda b, i: (b * nb + i, 0)),
        out_shape=jax.ShapeDtypeStruct((m, A_WIDTH), BF16),
        scratch_shapes=[pltpu.VMEM((A_HEADS, A_D, A_D), F32)] + 2 * [
            pltpu.VMEM((2 * N_LEV + 2, CHUNK, A_WIDTH), BF16),
            pltpu.VMEM((CHUNK, A_WIDTH), F32),
            pltpu.VMEM((8, A_WIDTH), F32)],
        compiler_params=_cparams(("arbitrary", "arbitrary")),
        name="hgrn",
    )(proj, proj, proj, proj, gain.reshape(1, A_D), jnp.asarray(_hgrn_masks()))


def _band_kernel(q_ref, k0_ref, k1_ref, k2_ref, v0_ref, v1_ref, v2_ref, g_ref, bias_ref, y_ref):
    nt = (((1,), (1,)), ((), ()))
    k_refs = (k0_ref, k1_ref, k2_ref)
    v_refs = (v0_ref, v1_ref, v2_ref)
    ones = jnp.ones((3 * TB, B_DH), BF16)

    def scores(h):
        q = q_ref[h]
        return [lax.dot_general(q, k_refs[j][h], nt, preferred_element_type=F32)
                + bias_ref[0, h, :, j * TB:(j + 1) * TB] for j in range(3)]

    def attend(h, s):
        mx = jnp.max(jnp.maximum(jnp.maximum(s[0], s[1]), s[2]), axis=-1, keepdims=True)
        p = jnp.concatenate([jnp.exp2(sj - mx).astype(BF16) for sj in s], axis=1)
        v = jnp.concatenate([v_refs[0][h], v_refs[1][h], v_refs[2][h]], axis=0)
        o = jnp.dot(p, jnp.concatenate([v, ones], axis=1), preferred_element_type=F32)
        y_ref[:, h * B_DH:(h + 1) * B_DH] = (o[:, :B_DH] / o[:, B_DH:]).astype(BF16) * g_ref[h]

    s_next = scores(0)
    for h in range(B_HEADS):
        s_cur = s_next
        if h + 1 < B_HEADS:
            s_next = scores(h + 1)
        attend(h, s_cur)


def _toeplitz(u):
    hh = u.shape[0]
    ext = jnp.concatenate([u[:, ::-1], jnp.zeros((hh, 1), u.dtype)], axis=1)
    win = jnp.tile(ext, (1, CHUNK + 2))[:, :CHUNK * (LANE + 1)]
    win = win.reshape(hh, CHUNK, LANE + 1)[:, :, :CHUNK]
    return win[:, ::-1, :]


def _band_bias(rel_bias):
    rb = rel_bias.astype(F32) * LOG2E
    hh = rb.shape[0]
    last = rb[:, 2 * REL_CLIP:]
    far = jnp.broadcast_to(last[:, :, None], (hh, CHUNK, CHUNK))
    tile = {
        B_PREV: _toeplitz(rb[:, REL_CLIP - 63:REL_CLIP + 64]),
        B_PREV - 1: _toeplitz(rb[:, REL_CLIP + 1:2 * REL_CLIP]),
        B_PREV - 2: _toeplitz(jnp.concatenate(
            [rb[:, REL_CLIP + 65:], jnp.broadcast_to(last, (hh, CHUNK - 1))], axis=1)),
    }
    neg = jnp.full((hh, CHUNK, CHUNK), NEG_INF, F32)
    rows = []
    for cq in range(TB // CHUNK):
        cols = []
        for w in range(3 * TB // CHUNK):
            jj = w - cq
            cols.append(neg if (jj < 0 or jj > B_PREV) else tile.get(jj, far))
        rows.append(jnp.concatenate(cols, axis=2))
    full = jnp.concatenate(rows, axis=1)
    col = np.arange(3 * TB)[None, None, :]
    step0 = jnp.where(col < 2 * TB, NEG_INF, full)
    step1 = jnp.where(col < TB, NEG_INF, full)
    return jnp.stack([step0, step1, full])


def _band(proj, bias, bsz, seq):
    m = bsz * seq
    nb = seq // TB
    cur = lambda blk: pl.BlockSpec((B_HEADS, TB, LANE), lambda b, i, blk=blk: (blk, b * nb + i, 0))
    prev = lambda blk, d: pl.BlockSpec(
        (B_HEADS, TB, LANE), lambda b, i, blk=blk, d=d: (blk, b * nb + jnp.maximum(i - d, 0), 0))
    return pl.pallas_call(
        _band_kernel,
        grid=(bsz, nb),
        in_specs=[cur(SLOT_BQ),
                  prev(SLOT_BK, 2), prev(SLOT_BK, 1), cur(SLOT_BK),
                  prev(SLOT_BV, 2), prev(SLOT_BV, 1), cur(SLOT_BV),
                  cur(SLOT_BG),
                  pl.BlockSpec((1, B_HEADS, TB, 3 * TB),
                               lambda b, i: (jnp.minimum(i, 2), 0, 0, 0))],
        out_specs=pl.BlockSpec((TB, B_WIDTH), lambda b, i: (b * nb + i, 0)),
        out_shape=jax.ShapeDtypeStruct((m, B_WIDTH), BF16),
        compiler_params=_cparams(("arbitrary", "arbitrary")),
        name="band",
    )(proj, proj, proj, proj, proj, proj, proj, proj, bias)


def _ret_kernel(qk_ref, v_ref, g_ref, cos_ref, sin_ref, dmask_ref, qdec_ref, kdec_ref,
                kin_ref, cdec_ref, y_ref, st_ref):
    @pl.when(pl.program_id(1) == 0)
    def _():
        st_ref[...] = jnp.zeros_like(st_ref)

    nt = (((1,), (1,)), ((), ()))
    tn = (((0,), (0,)), ((), ()))
    cos = cos_ref[...]
    sin = sin_ref[...]
    lane = lax.broadcasted_iota(jnp.int32, (TC, LANE), 1)
    first_half = (lane % C_DK) < (C_DK // 2)

    def rotary(x):
        swapped = jnp.where(first_half, pltpu.roll(x, LANE - C_DK // 2, 1),
                            pltpu.roll(x, C_DK // 2, 1))
        return x * cos + swapped * sin

    rot = [rotary(qk_ref[t].astype(F32)) for t in range(C_QK_TILES)]
    qrot = rot[:3]
    krot = [pltpu.roll(x, C_DK, 1) for x in rot[2:]]
    heads = range(C_HEADS)
    qb = [x.astype(BF16) for x in qrot]
    s = [lax.dot_general(qb[h // 2], (krot[(h + 1) // 2] * kin_ref[h]).astype(BF16), nt,
                         preferred_element_type=F32) for h in heads]
    st = [st_ref[h] for h in heads]
    o_in = [jnp.dot((qrot[h // 2] * qdec_ref[h]).astype(BF16), st[h].astype(BF16),
                    preferred_element_type=F32) for h in heads]
    for h in heads:
        st_ref[h] = st[h] * cdec_ref[h] + lax.dot_general(
            (krot[(h + 1) // 2] * kdec_ref[h]).astype(BF16), v_ref[h], tn,
            preferred_element_type=F32)
    for h in heads:
        o = o_in[h] + jnp.dot((s[h] * dmask_ref[h]).astype(BF16), v_ref[h],
                              preferred_element_type=F32)
        o = o * lax.rsqrt(jnp.mean(o * o, axis=-1, keepdims=True) + EPS)
        y_ref[:, h * C_DV:(h + 1) * C_DV] = o.astype(BF16) * g_ref[h]


def _ret_consts():
    hh = np.arange(C_HEADS, dtype=np.float64)
    log_gamma = np.log1p(-np.exp2(-5.0 - hh))
    pos = np.arange(TC, dtype=np.float64)
    rel = pos[:, None] - pos[None, :]
    dmask = np.where(rel >= 0, np.exp(log_gamma[:, None, None] * np.maximum(rel, 0.0)), 0.0)
    lane = np.arange(LANE)
    head_lanes = np.stack([(lane // C_DK) == (h % 2) for h in range(C_HEADS)]).astype(np.float64)
    qdec = np.exp(log_gamma[:, None] * (pos + 1.0)[None, :])[:, :, None] * head_lanes[:, None, :]
    kdec = (np.exp(log_gamma[:, None] * (TC - 1 - pos)[None, :])[:, :, None]
            * head_lanes[:, None, :] * C_DK ** -0.5)
    kin = head_lanes[:, None, :] * C_DK ** -0.5
    cdec = np.broadcast_to(np.exp(log_gamma * TC)[:, None, None], (C_HEADS, 1, C_DV))
    f = lambda a: jnp.asarray(np.ascontiguousarray(a), dtype=F32)
    return f(dmask), f(qdec), f(kdec), f(kin), f(cdec)


def _rope_tables(seq):
    inv_freq = ROPE_BASE ** (-jnp.linspace(0.0, 1.0, C_DK // 2, dtype=F32))
    ang = jnp.arange(seq, dtype=F32)[:, None] * inv_freq[None, :]
    cos, sin = jnp.cos(ang), jnp.sin(ang)
    cos_t = jnp.tile(cos, (1, LANE // (C_DK // 2)))
    sin_t = jnp.tile(jnp.concatenate([-sin, sin], axis=1), (1, LANE // C_DK))
    return cos_t, sin_t


def _ret(proj, cos_t, sin_t, consts, bsz, seq):
    m = bsz * seq
    nb = seq // TC
    dmask, qdec, kdec, kin, cdec = consts
    spec = lambda blk: pl.BlockSpec((C_HEADS, TC, LANE), lambda b, i, blk=blk: (blk, b * nb + i, 0))
    const = lambda shape: pl.BlockSpec(shape, lambda b, i: (0,) * len(shape))
    return pl.pallas_call(
        _ret_kernel,
        grid=(bsz, nb),
        in_specs=[spec(SLOT_CQK), spec(SLOT_CV), spec(SLOT_CG),
                  pl.BlockSpec((TC, LANE), lambda b, i: (i, 0)),
                  pl.BlockSpec((TC, LANE), lambda b, i: (i, 0)),
                  const(dmask.shape), const(qdec.shape), const(kdec.shape),
                  const(kin.shape), const(cdec.shape)],
        out_specs=pl.BlockSpec((TC, C_WIDTH), lambda b, i: (b * nb + i, 0)),
        out_shape=jax.ShapeDtypeStruct((m, C_WIDTH), BF16),
        scratch_shapes=[pltpu.VMEM((C_HEADS, LANE, C_DV), F32)],
        compiler_params=_cparams(("arbitrary", "arbitrary")),
        name="retention",
    )(proj, proj, proj, cos_t, sin_t, dmask, qdec, kdec, kin, cdec)


def _outproj_kernel(ya_ref, yb_ref, yc_ref, x_ref, w_ref, g_ref, *out_refs, last):
    for r in range(0, TM_OUT, TM_OUT_SUB):
        rows = slice(r, r + TM_OUT_SUB)
        y = jnp.concatenate([ya_ref[rows, :], yb_ref[rows, :], yc_ref[rows, :]], axis=-1)
        xn = x_ref[rows, :] + jnp.dot(y, w_ref[...], preferred_element_type=F32)
        ms = jnp.mean(xn * xn, axis=-1, keepdims=True)
        hn = xn * lax.rsqrt(ms + EPS) * g_ref[...]
        if last:
            out_refs[0][rows, :] = hn
        else:
            out_refs[0][rows, :] = xn
            out_refs[1][rows, :] = hn.astype(BF16)


def _outproj(ya, yb, yc, x2, w, gain, last):
    m = x2.shape[0]
    row = lambda width: pl.BlockSpec((TM_OUT, width), lambda i: (i, 0))
    if last:
        out_specs = row(D_MODEL)
        out_shape = jax.ShapeDtypeStruct((m, D_MODEL), F32)
    else:
        out_specs = [row(D_MODEL), row(D_MODEL)]
        out_shape = [jax.ShapeDtypeStruct((m, D_MODEL), F32),
                     jax.ShapeDtypeStruct((m, D_MODEL), BF16)]
    return pl.pallas_call(
        functools.partial(_outproj_kernel, last=last),
        grid=(m // TM_OUT,),
        in_specs=[row(A_WIDTH), row(B_WIDTH), row(C_WIDTH), row(D_MODEL),
                  pl.BlockSpec((D_MIX, D_MODEL), lambda i: (0, 0)),
                  pl.BlockSpec((1, D_MODEL), lambda i: (0, 0))],
        out_specs=out_specs,
        out_shape=out_shape,
        compiler_params=_cparams(("arbitrary",)),
        name="outproj_last" if last else "outproj",
    )(ya, yb, yc, x2, w, gain.reshape(1, D_MODEL))


def _prep_w_in(w):
    scale = np.ones((1, w.shape[1]), np.float32)
    scale[:, 4 * A_WIDTH:4 * A_WIDTH + B_WIDTH] = B_DH ** -0.5 * LOG2E
    return (w * scale).astype(BF16)


def kernel(x, w_in, norm_gain, lb_logits, hgrn_norm_gain, rel_bias, w_out, final_gain):
    bsz, seq, _ = x.shape
    m = bsz * seq
    lb_all = jnp.cumsum(jax.nn.softmax(lb_logits.astype(F32), axis=0), axis=0)
    lb_all = lb_all - lb_all[0:1]
    cos_t, sin_t = _rope_tables(seq)
    ret_consts = _ret_consts()
    x2 = x.reshape(m, D_MODEL)
    h = None
    for layer in range(DEPTH):
        w = _prep_w_in(w_in[layer])
        lb = lb_all[layer]
        proj = _inproj(x2, w, lb, norm_gain[0]) if layer == 0 else _inproj(h, w, lb)
        ya = _hgrn(proj, hgrn_norm_gain[layer], bsz, seq)
        yb = _band(proj, _band_bias(rel_bias[layer]), bsz, seq)
        yc = _ret(proj, cos_t, sin_t, ret_consts, bsz, seq)
        last = layer == DEPTH - 1
        gain = final_gain if last else norm_gain[layer + 1]
        res = _outproj(ya, yb, yc, x2, w_out[layer].astype(BF16), gain, last)
        if last:
            x2 = res
        else:
            x2, h = res
    return x2.reshape(bsz, seq, D_MODEL)
```

```python
import functools

import numpy as np
import jax
import jax.numpy as jnp
from jax import lax
from jax.experimental import pallas as pl
from jax.experimental.pallas import tpu as pltpu

F32 = jnp.float32
BF16 = jnp.bfloat16

D_MODEL = 2048
DEPTH = 2
CHUNK = 64
EPS = 1e-6
NEG_INF = -1e30
LOG2E = 1.4426950408889634
LANE = 128
SUBLANE = 8

A_HEADS, A_D = 6, 128
A_WIDTH = A_HEADS * A_D
B_HEADS, B_DH = 5, 128
B_WIDTH = B_HEADS * B_DH
B_PREV = 8
REL_CLIP = 128
C_HEADS, C_DK, C_DV = 5, 64, 128
C_QK = C_HEADS * C_DK
C_WIDTH = C_HEADS * C_DV
ROPE_BASE = 10000.0
D_MIX = A_WIDTH + B_WIDTH + C_WIDTH
D_IN_PAD = 7680
N_SLOTS = D_IN_PAD // LANE

SLOT_AQ, SLOT_AF, SLOT_AI, SLOT_AG = 0, 1, 2, 3
SLOT_BQ, SLOT_BK, SLOT_BV, SLOT_BG = 5, 6, 7, 8
SLOT_CQK, SLOT_CV, SLOT_CG = 9, 10, 11
C_QK_TILES = 2 * C_QK // LANE
SLOT_KIND = (["silu"] * A_HEADS + ["keep"] * A_HEADS + ["none"] * A_HEADS + ["silu"] * A_HEADS
             + ["none"] * (1 + 3 * B_HEADS) + ["silu"] * B_HEADS
             + ["none"] * (C_QK_TILES + C_HEADS) + ["silu"] * C_HEADS)

TM_PROJ = 1024
SLOTS_PER_STEP = 20
SPARE_SLOT = A_HEADS * 4
TN_SUB = 512
TM_OUT = 512
TA = 2048
A_CHUNK = 64
TB = 256
TC = 256
TBC_STEP = 512
A_LEVELS = (1, 2, 4, 8, 16, 32)
N_LEV = len(A_LEVELS)
VMEM_LIMIT = 56 * 1024 * 1024
DEC_ROWS = SUBLANE


def _row_vec(v):
    return v.reshape(1, -1)


def _cparams(sem):
    return pltpu.CompilerParams(dimension_semantics=sem, vmem_limit_bytes=VMEM_LIMIT)


def _activate(t, slot, lb_ref):
    kind = SLOT_KIND[slot]
    if kind == "silu":
        half = 0.5 * t
        return half + half * jnp.tanh(half)
    if kind == "keep":
        head = slot - SLOT_AF * A_HEADS
        oml = 1.0 - lb_ref[0:1, head * A_D:(head + 1) * A_D]
        e = jnp.exp(-jnp.abs(t))
        rcp = 1.0 / (1.0 + e)
        return oml * jnp.where(t >= 0.0, e * rcp, rcp)
    return t


def _w_slot_start(j):
    return j * SLOTS_PER_STEP - (j > SPARE_SLOT // SLOTS_PER_STEP)


def _dot_groups(jj):
    first = jj * SLOTS_PER_STEP
    w0 = int(_w_slot_start(jj))
    runs = [(first, first + SLOTS_PER_STEP)]
    if first <= SPARE_SLOT < first + SLOTS_PER_STEP:
        runs = [(first, SPARE_SLOT), (SPARE_SLOT + 1, first + SLOTS_PER_STEP)]
    groups = []
    for lo, hi in runs:
        for s in range(lo, hi, TN_SUB // LANE):
            n = min(TN_SUB // LANE, hi - s)
            groups.append((s, s - (s > SPARE_SLOT) - w0, n))
    return groups


def _project(h_ref, w_ref, lb_ref, o_ref):
    for jj in range(N_SLOTS // SLOTS_PER_STEP):
        @pl.when(pl.program_id(1) == jj)
        def _(jj=jj):
            h = h_ref[...]
            first = jj * SLOTS_PER_STEP
            if first <= SPARE_SLOT < first + SLOTS_PER_STEP:
                o_ref[SPARE_SLOT - first] = jnp.zeros((TM_PROJ, LANE), BF16)
            cost = {"keep": 0, "silu": 1, "none": 2}
            order = lambda g: min(cost[SLOT_KIND[g[0] + k]] for k in range(g[2]))
            for s0, wt, n in sorted(_dot_groups(jj), key=order):
                acc = jnp.dot(h, w_ref[:, wt * LANE:(wt + n) * LANE], preferred_element_type=F32)
                for k in range(n):
                    t = _activate(acc[:, k * LANE:(k + 1) * LANE], s0 + k, lb_ref)
                    o_ref[s0 - first + k] = t.astype(BF16)


def _inproj_kernel(h_ref, w_ref, lb_ref, o_ref):
    _project(h_ref, w_ref, lb_ref, o_ref)


def _inproj_norm_kernel(x_ref, g_ref, w_ref, lb_ref, o_ref, h_ref):
    @pl.when(pl.program_id(1) == 0)
    def _():
        x = x_ref[...]
        ms = jnp.mean(x * x, axis=-1, keepdims=True)
        h_ref[...] = (x * lax.rsqrt(ms + EPS) * g_ref[0:1, :]).astype(BF16)

    _project(h_ref, w_ref, lb_ref, o_ref)


def _inproj(h, w, lb, gain=None):
    m = h.shape[0]
    tm = TM_PROJ
    rows = pl.BlockSpec((tm, D_MODEL), lambda i, j: (i, 0))
    after_spare = SPARE_SLOT // SLOTS_PER_STEP
    wspec = pl.BlockSpec(
        (pl.Element(D_MODEL), pl.Element(SLOTS_PER_STEP * LANE)),
        lambda i, j: (0, (j * SLOTS_PER_STEP - jnp.where(j > after_spare, 1, 0)) * LANE))
    common = dict(
        grid=(m // tm, N_SLOTS // SLOTS_PER_STEP),
        out_specs=pl.BlockSpec((SLOTS_PER_STEP, tm, LANE), lambda i, j: (j, i, 0)),
        out_shape=jax.ShapeDtypeStruct((N_SLOTS, m, LANE), BF16),
        compiler_params=_cparams(("arbitrary", "arbitrary")),
    )
    lbspec = pl.BlockSpec((1, A_WIDTH), lambda i, j: (0, 0))
    lb = _row_vec(lb)
    if gain is None:
        return pl.pallas_call(_inproj_kernel, in_specs=[rows, wspec, lbspec], name="inproj",
                              **common)(h, w, lb)
    return pl.pallas_call(
        _inproj_norm_kernel,
        in_specs=[rows, pl.BlockSpec((1, D_MODEL), lambda i, j: (0, 0)), wspec, lbspec],
        scratch_shapes=[pltpu.VMEM((TM_PROJ, D_MODEL), BF16)],
        name="inproj_norm", **common)(h, _row_vec(gain), w, lb)


def _row(x, r):
    return jnp.broadcast_to(x[r:r + 1, :], (SUBLANE, x.shape[1]))


def _decay_products(f):
    n, w = f.shape
    ng = n // SUBLANE
    r = lax.broadcasted_iota(jnp.int32, (SUBLANE, w), 0)
    b0, b1, b2 = (r & 1) == 1, (r & 2) == 2, (r & 4) == 4
    lo = r < 4
    one = jnp.ones((SUBLANE, w), F32)
    groups = [f[SUBLANE * j:SUBLANE * (j + 1), :] for j in range(ng)]
    p = {1: groups, 2: [], 4: [], 8: []}
    rr = {1: [one] * ng, 2: [], 4: [], 8: []}
    for x1 in groups:
        x2 = x1 * jnp.where(b0, pltpu.roll(x1, 1, 0), one)
        x4 = x2 * jnp.where(b1, jnp.where(lo, _row(x2, 1), _row(x2, 5)), one)
        x8 = x4 * jnp.where(b2, _row(x4, 3), one)
        r2 = jnp.where(b0, one, pltpu.roll(x1, 7, 0))
        r4 = r2 * jnp.where(b1, one, jnp.where(lo, _row(x2, 3), _row(x2, 7)))
        r8 = r4 * jnp.where(b2, one, _row(x4, 7))
        p[2].append(x2); p[4].append(x4); p[8].append(x8)
        rr[2].append(r2); rr[4].append(r4); rr[8].append(r8)
    m = SUBLANE
    while m < n:
        g = m // SUBLANE
        tot = [_row(p[m][g * b + g - 1], SUBLANE - 1) for b in range(n // m)]
        pn, rn = [], []
        for j in range(ng):
            b = j // g
            pn.append(p[m][j] * tot[b - 1] if b % 2 == 1 else p[m][j])
            rn.append(rr[m][j] * tot[b + 1] if b % 2 == 0 else rr[m][j])
        p[2 * m], rr[2 * m] = pn, rn
        m *= 2
    cat = lambda xs: jnp.concatenate(xs, axis=0)
    return {m: (cat(p[m]), cat(rr[m])) for m in p}


def _hgrn_kernel(q_ref, k_ref, i_ref, g_ref, gain_ref, mask_ref, y_ref, st_ref,
                 lev0, aux0, dec0, lev1, aux1, dec1):
    @pl.when(pl.program_id(1) == 0)
    def _():
        st_ref[...] = jnp.zeros_like(st_ref)

    gain = gain_ref[0:1, :]
    nt = (((1,), (1,)), ((), ()))
    tn = (((0,), (0,)), ((), ()))
    bufs = ((lev0, aux0, dec0), (lev1, aux1, dec1))
    n_groups = A_CHUNK // SUBLANE

    def prepare(c, buf):
        lev, aux, dec = buf
        r0 = pl.multiple_of(c * A_CHUNK, A_CHUNK)

        def load(ref):
            return jnp.concatenate(
                [ref[h, pl.ds(r0, A_CHUNK), :] for h in range(A_HEADS)], axis=-1)

        qb, kb = load(q_ref), load(k_ref)
        qs, kk, vv = qb.astype(F32), kb.astype(F32), load(i_ref).astype(F32)
        prod = _decay_products(1.0 - kk)
        for li, m in enumerate(A_LEVELS):
            lev[li] = qb * prod[m][0].astype(BF16)
            lev[N_LEV + li] = kb if m == 1 else kb * prod[m][1].astype(BF16)
        lev[2 * N_LEV] = qb * prod[A_CHUNK][0].astype(BF16)
        lev[2 * N_LEV + 1] = kb * prod[A_CHUNK][1].astype(BF16)
        dec[0:1, :] = prod[A_CHUNK][0][A_CHUNK - 1:A_CHUNK, :]
        qk = qs * kk
        aux[...] = jnp.concatenate(
            [jnp.sum(qk[:, h * A_D:(h + 1) * A_D], axis=-1, keepdims=True)
             * vv[:, h * A_D:(h + 1) * A_D] for h in range(A_HEADS)], axis=-1)

    def mix(c, buf):
        lev, aux, dec = buf
        r0 = pl.multiple_of(c * A_CHUNK, A_CHUNK)
        heads = range(A_HEADS)
        sl = [slice(h * A_D, (h + 1) * A_D) for h in heads]
        s_tot = []
        for h in heads:
            rows = [jnp.zeros((SUBLANE, A_CHUNK), F32)] * n_groups
            for li, m in enumerate(A_LEVELS):
                s = lax.dot_general(lev[li, :, sl[h]], lev[N_LEV + li, :, sl[h]], nt,
                                    preferred_element_type=F32)
                for j in range(n_groups):
                    if m < SUBLANE or (SUBLANE * j // m) % 2 == 1:
                        rg = slice(SUBLANE * j, SUBLANE * (j + 1))
                        rows[j] = rows[j] + s[rg, :] * mask_ref[li, rg, :]
            s_tot.append(jnp.concatenate(rows, axis=0).astype(BF16))
        st = [st_ref[h] for h in heads]
        vb = [i_ref[h, pl.ds(r0, A_CHUNK), :] for h in heads]
        o_in = [lax.dot_general(lev[2 * N_LEV, :, sl[h]], st[h].astype(BF16), nt,
                                preferred_element_type=F32) for h in heads]
        for h in heads:
            st_ref[h] = st[h] * dec[0:1, sl[h]] + lax.dot_general(
                vb[h], lev[2 * N_LEV + 1, :, sl[h]], tn, preferred_element_type=F32)
        for h in heads:
            o = jnp.dot(s_tot[h], vb[h], preferred_element_type=F32) + o_in[h] + aux[:, sl[h]]
            o = o * lax.rsqrt(jnp.mean(o * o, axis=-1, keepdims=True) + EPS * A_D) * gain
            y_ref[pl.ds(r0, A_CHUNK), sl[h]] = o.astype(BF16) * g_ref[h, pl.ds(r0, A_CHUNK), :]

    n_chunks = TA // A_CHUNK
    prepare(0, bufs[0])

    def pair(k, carry):
        c = 2 * k
        prepare(c + 1, bufs[1])
        mix(c, bufs[0])
        prepare(c + 2, bufs[0])
        mix(c + 1, bufs[1])
        return carry

    lax.fori_loop(0, n_chunks // 2 - 1, pair, 0)
    prepare(n_chunks - 1, bufs[1])
    mix(n_chunks - 2, bufs[0])
    mix(n_chunks - 1, bufs[1])


def _hgrn_masks():
    t = np.arange(A_CHUNK)[:, None]
    s = np.arange(A_CHUNK)[None, :]
    out = []
    for m in A_LEVELS:
        ok = (t // (2 * m) == s // (2 * m)) & ((t // m) % 2 == 1) & ((s // m) % 2 == 0)
        out.append(ok)
    return np.stack(out).astype(np.float32)


def _hgrn(proj, gain, bsz, seq):
    m = bsz * seq
    nb = seq // TA
    spec = lambda blk: pl.BlockSpec((A_HEADS, TA, LANE), lambda b, i, blk=blk: (blk, b * nb + i, 0))
    return pl.pallas_call(
        _hgrn_kernel,
        grid=(bsz, nb),
        in_specs=[spec(SLOT_AQ), spec(SLOT_AF), spec(SLOT_AI), spec(SLOT_AG),
                  pl.BlockSpec((1, A_D), lambda b, i: (0, 0)),
                  pl.BlockSpec((N_LEV, A_CHUNK, A_CHUNK), lambda b, i: (0, 0, 0))],
        out_specs=pl.BlockSpec((TA, A_WIDTH), lambda b, i: (b * nb + i, 0)),
        out_shape=jax.ShapeDtypeStruct((m, A_WIDTH), BF16),
        scratch_shapes=[pltpu.VMEM((A_HEADS, A_D, A_D), F32)] + 2 * [
            pltpu.VMEM((2 * N_LEV + 2, A_CHUNK, A_WIDTH), BF16),
            pltpu.VMEM((A_CHUNK, A_WIDTH), F32),
            pltpu.VMEM((DEC_ROWS, A_WIDTH), F32)],
        compiler_params=_cparams(("arbitrary", "arbitrary")),
        name="hgrn",
    )(proj, proj, proj, proj, _row_vec(gain), jnp.asarray(_hgrn_masks()))


def _bandret_kernel(q_ref, kp2_ref, kp1_ref, kc_ref, vp2_ref, vp1_ref, vc_ref, bg_ref,
                    bias0_ref, bias1_ref, cqk_ref, cv_ref, cg_ref, cos_ref, sin_ref, dmask_ref,
                    qdec_ref, kdec_ref, kin_ref, cdec_ref, yb_ref, yc_ref, st_ref):
    @pl.when(pl.program_id(1) == 0)
    def _():
        st_ref[...] = jnp.zeros_like(st_ref)

    nt = (((1,), (1,)), ((), ()))
    tn = (((0,), (0,)), ((), ()))
    lane = lax.broadcasted_iota(jnp.int32, (TC, LANE), 1)
    first_half = (lane % C_DK) < (C_DK // 2)
    ones = jnp.ones((3 * TB, B_DH), BF16)
    cheads = range(C_HEADS)
    bias_refs = (bias0_ref, bias1_ref)

    def window(prev2, prev1, cur, u, h):
        blocks = [prev2[h], prev1[h]] + [cur[h, v * TB:(v + 1) * TB, :] for v in range(u + 1)]
        return blocks[-3:]

    for u in range(TBC_STEP // TB):
        rows = slice(u * TB, (u + 1) * TB)

        cos = cos_ref[rows, :]
        sin = sin_ref[rows, :]

        def rotary(x):
            swapped = jnp.where(first_half, pltpu.roll(x, LANE - C_DK // 2, 1),
                                pltpu.roll(x, C_DK // 2, 1))
            return x * cos + swapped * sin

        rot = [rotary(cqk_ref[t, rows, :].astype(F32)) for t in range(C_QK_TILES)]
        qrot = rot[:3]
        krot = [pltpu.roll(x, C_DK, 1) for x in rot[2:]]
        qb = [x.astype(BF16) for x in qrot]
        cv = [cv_ref[h, rows, :] for h in cheads]
        cs = [lax.dot_general(qb[h // 2],
                              (krot[(h + 1) // 2] * kin_ref[h:h + 1, :]).astype(BF16), nt,
                              preferred_element_type=F32) for h in cheads]
        st = [st_ref[h] for h in cheads]
        o_in = [jnp.dot((qrot[h // 2] * qdec_ref[h]).astype(BF16), st[h].astype(BF16),
                        preferred_element_type=F32) for h in cheads]
        for h in cheads:
            st_ref[h] = st[h] * cdec_ref[h:h + 1, :] + lax.dot_general(
                (krot[(h + 1) // 2] * kdec_ref[h]).astype(BF16), cv[h], tn,
                preferred_element_type=F32)

        def scores(h):
            q = q_ref[h, rows, :]
            ks = window(kp2_ref, kp1_ref, kc_ref, u, h)
            return [lax.dot_general(q, ks[j], nt, preferred_element_type=F32)
                    + bias_refs[u][0, h, :, j * TB:(j + 1) * TB] for j in range(3)]

        def attend(h, s):
            mx = jnp.max(jnp.maximum(jnp.maximum(s[0], s[1]), s[2]), axis=-1, keepdims=True)
            p = jnp.concatenate([jnp.exp2(sj - mx).astype(BF16) for sj in s], axis=1)
            v = jnp.concatenate(window(vp2_ref, vp1_ref, vc_ref, u, h), axis=0)
            o = jnp.dot(p, jnp.concatenate([v, ones], axis=1), preferred_element_type=F32)
            yb_ref[rows, h * B_DH:(h + 1) * B_DH] = (
                (o[:, :B_DH] / o[:, B_DH:]).astype(BF16) * bg_ref[h, rows, :])

        s_next = scores(0)
        for h in cheads:
            o = o_in[h] + jnp.dot((cs[h] * dmask_ref[h]).astype(BF16), cv[h],
                                  preferred_element_type=F32)
            o = o * lax.rsqrt(jnp.mean(o * o, axis=-1, keepdims=True) + EPS)
            yc_ref[rows, h * C_DV:(h + 1) * C_DV] = o.astype(BF16) * cg_ref[h, rows, :]
        for h in range(B_HEADS):
            s_cur = s_next
            if h + 1 < B_HEADS:
                s_next = scores(h + 1)
            attend(h, s_cur)


def _toeplitz(u):
    hh = u.shape[0]
    ext = jnp.concatenate([u[:, ::-1], jnp.zeros((hh, 1), u.dtype)], axis=1)
    win = jnp.tile(ext, (1, CHUNK + 2))[:, :CHUNK * (LANE + 1)]
    win = win.reshape(hh, CHUNK, LANE + 1)[:, :, :CHUNK]
    return win[:, ::-1, :]


def _band_bias(rel_bias):
    rb = rel_bias.astype(F32) * LOG2E
    hh = rb.shape[0]
    last = rb[:, 2 * REL_CLIP:]
    far = jnp.broadcast_to(last[:, :, None], (hh, CHUNK, CHUNK))
    tile = {
        B_PREV: _toeplitz(rb[:, REL_CLIP - 63:REL_CLIP + 64]),
        B_PREV - 1: _toeplitz(rb[:, REL_CLIP + 1:2 * REL_CLIP]),
        B_PREV - 2: _toeplitz(jnp.concatenate(
            [rb[:, REL_CLIP + 65:], jnp.broadcast_to(last, (hh, CHUNK - 1))], axis=1)),
    }
    neg = jnp.full((hh, CHUNK, CHUNK), NEG_INF, F32)
    rows = []
    for cq in range(TB // CHUNK):
        cols = []
        for w in range(3 * TB // CHUNK):
            jj = w - cq
            cols.append(neg if (jj < 0 or jj > B_PREV) else tile.get(jj, far))
        rows.append(jnp.concatenate(cols, axis=2))
    full = jnp.concatenate(rows, axis=1)
    col = np.arange(3 * TB)[None, None, :]
    step0 = jnp.where(col < 2 * TB, NEG_INF, full)
    step1 = jnp.where(col < TB, NEG_INF, full)
    return jnp.stack([step0, step1, full])


def _ret_consts():
    hh = np.arange(C_HEADS, dtype=np.float64)
    log_gamma = np.log1p(-np.exp2(-5.0 - hh))
    pos = np.arange(TC, dtype=np.float64)
    rel = pos[:, None] - pos[None, :]
    dmask = np.where(rel >= 0, np.exp(log_gamma[:, None, None] * np.maximum(rel, 0.0)), 0.0)
    lane = np.arange(LANE)
    head_lanes = np.stack([(lane // C_DK) == (h % 2) for h in range(C_HEADS)]).astype(np.float64)
    qdec = np.exp(log_gamma[:, None] * (pos + 1.0)[None, :])[:, :, None] * head_lanes[:, None, :]
    kdec = (np.exp(log_gamma[:, None] * (TC - 1 - pos)[None, :])[:, :, None]
            * head_lanes[:, None, :] * C_DK ** -0.5)
    kin = head_lanes * C_DK ** -0.5
    cdec = np.broadcast_to(np.exp(log_gamma * TC)[:, None], (C_HEADS, C_DV))
    f = lambda a: jnp.asarray(np.ascontiguousarray(a), dtype=F32)
    return f(dmask), f(qdec), f(kdec), f(kin), f(cdec)


def _rope_tables(seq):
    inv_freq = ROPE_BASE ** (-jnp.linspace(0.0, 1.0, C_DK // 2, dtype=F32))
    ang = jnp.arange(seq, dtype=F32)[:, None] * inv_freq[None, :]
    cos, sin = jnp.cos(ang), jnp.sin(ang)
    cos_t = jnp.tile(cos, (1, LANE // (C_DK // 2)))
    sin_t = jnp.tile(jnp.concatenate([-sin, sin], axis=1), (1, LANE // C_DK))
    return cos_t, sin_t


def _bandret(proj, bias, cos_t, sin_t, consts, bsz, seq):
    assert TB == TC and TBC_STEP == 2 * TB
    m = bsz * seq
    nb = seq // TBC_STEP
    nsub = seq // TB
    dmask, qdec, kdec, kin, cdec = consts
    cur = lambda blk: pl.BlockSpec(
        (B_HEADS, TBC_STEP, LANE), lambda b, i, blk=blk: (blk, b * nb + i, 0))
    prev = lambda blk, d: pl.BlockSpec(
        (B_HEADS, TB, LANE),
        lambda b, i, blk=blk, d=d: (blk, b * nsub + jnp.maximum(2 * i - d, 0), 0))
    bias_of = lambda u: pl.BlockSpec(
        (1, B_HEADS, TB, 3 * TB), lambda b, i, u=u: (jnp.minimum(2 * i + u, 2), 0, 0, 0))
    const = lambda shape: pl.BlockSpec(shape, lambda b, i: (0,) * len(shape))
    rows = lambda width: pl.BlockSpec((TBC_STEP, width), lambda b, i: (b * nb + i, 0))
    return pl.pallas_call(
        _bandret_kernel,
        grid=(bsz, nb),
        in_specs=[cur(SLOT_BQ),
                  prev(SLOT_BK, 2), prev(SLOT_BK, 1), cur(SLOT_BK),
                  prev(SLOT_BV, 2), prev(SLOT_BV, 1), cur(SLOT_BV),
                  cur(SLOT_BG), bias_of(0), bias_of(1),
                  cur(SLOT_CQK), cur(SLOT_CV), cur(SLOT_CG),
                  pl.BlockSpec((TBC_STEP, LANE), lambda b, i: (i, 0)),
                  pl.BlockSpec((TBC_STEP, LANE), lambda b, i: (i, 0)),
                  const(dmask.shape), const(qdec.shape), const(kdec.shape),
                  const(kin.shape), const(cdec.shape)],
        out_specs=[rows(B_WIDTH), rows(C_WIDTH)],
        out_shape=[jax.ShapeDtypeStruct((m, B_WIDTH), BF16),
                   jax.ShapeDtypeStruct((m, C_WIDTH), BF16)],
        scratch_shapes=[pltpu.VMEM((C_HEADS, LANE, C_DV), F32)],
        compiler_params=_cparams(("arbitrary", "arbitrary")),
        name="bandret",
    )(*([proj] * 8), bias, bias, proj, proj, proj, cos_t, sin_t, dmask, qdec, kdec, kin, cdec)


def _outproj_kernel(ya_ref, yb_ref, yc_ref, x_ref, w_ref, g_ref, *out_refs, last):
    y = jnp.concatenate([ya_ref[...], yb_ref[...], yc_ref[...]], axis=-1)
    xn = x_ref[...] + jnp.dot(y, w_ref[...], preferred_element_type=F32)
    ms = jnp.mean(xn * xn, axis=-1, keepdims=True)
    hn = xn * lax.rsqrt(ms + EPS) * g_ref[0:1, :]
    if last:
        out_refs[0][...] = hn
    else:
        out_refs[0][...] = xn
        out_refs[1][...] = hn.astype(BF16)


def _outproj(ya, yb, yc, x2, w, gain, last):
    m = x2.shape[0]
    row = lambda width: pl.BlockSpec((TM_OUT, width), lambda i: (i, 0))
    if last:
        out_specs = row(D_MODEL)
        out_shape = jax.ShapeDtypeStruct((m, D_MODEL), F32)
    else:
        out_specs = [row(D_MODEL), row(D_MODEL)]
        out_shape = [jax.ShapeDtypeStruct((m, D_MODEL), F32),
                     jax.ShapeDtypeStruct((m, D_MODEL), BF16)]
    return pl.pallas_call(
        functools.partial(_outproj_kernel, last=last),
        grid=(m // TM_OUT,),
        in_specs=[row(A_WIDTH), row(B_WIDTH), row(C_WIDTH), row(D_MODEL),
                  pl.BlockSpec((D_MIX, D_MODEL), lambda i: (0, 0)),
                  pl.BlockSpec((1, D_MODEL), lambda i: (0, 0))],
        out_specs=out_specs,
        out_shape=out_shape,
        compiler_params=_cparams(("arbitrary",)),
        name="outproj_last" if last else "outproj",
    )(ya, yb, yc, x2, w, _row_vec(gain))


def _prep_w_in(w):
    scale = np.ones((1, w.shape[1]), np.float32)
    scale[:, 4 * A_WIDTH:4 * A_WIDTH + B_WIDTH] = B_DH ** -0.5 * LOG2E
    return (w * scale).astype(BF16)


def kernel(x, w_in, norm_gain, lb_logits, hgrn_norm_gain, rel_bias, w_out, final_gain):
    bsz, seq, _ = x.shape
    m = bsz * seq
    lb_all = jnp.cumsum(jax.nn.softmax(lb_logits.astype(F32), axis=0), axis=0)
    lb_all = lb_all - lb_all[0:1]
    cos_t, sin_t = _rope_tables(seq)
    ret_consts = _ret_consts()
    x2 = x.reshape(m, D_MODEL)
    h = None
    for layer in range(DEPTH):
        w = _prep_w_in(w_in[layer])
        lb = lb_all[layer]
        proj = _inproj(x2, w, lb, norm_gain[0]) if layer == 0 else _inproj(h, w, lb)
        ya = _hgrn(proj, hgrn_norm_gain[layer], bsz, seq)
        yb, yc = _bandret(proj, _band_bias(rel_bias[layer]), cos_t, sin_t, ret_consts, bsz, seq)
        last = layer == DEPTH - 1
        gain = final_gain if last else norm_gain[layer + 1]
        res = _outproj(ya, yb, yc, x2, w_out[layer].astype(BF16), gain, last)
        if last:
            x2 = res
        else:
            x2, h = res
    return x2.reshape(bsz, seq, D_MODEL)
```

```python
import functools

import numpy as np
import jax
import jax.numpy as jnp
from jax import lax
from jax.experimental import pallas as pl
from jax.experimental.pallas import tpu as pltpu

F32 = jnp.float32
BF16 = jnp.bfloat16

D_MODEL = 2048
DEPTH = 2
CHUNK = 64
EPS = 1e-6
NEG_INF = -1e30
LOG2E = 1.4426950408889634
LANE = 128
SUBLANE = 8

A_HEADS, A_D = 6, 128
A_WIDTH = A_HEADS * A_D
B_HEADS, B_DH = 5, 128
B_WIDTH = B_HEADS * B_DH
B_PREV = 8
REL_CLIP = 128
C_HEADS, C_DK, C_DV = 5, 64, 128
C_QK = C_HEADS * C_DK
C_WIDTH = C_HEADS * C_DV
ROPE_BASE = 10000.0
D_MIX = A_WIDTH + B_WIDTH + C_WIDTH
D_IN_PAD = 7680
N_SLOTS = D_IN_PAD // LANE

SLOT_AQ, SLOT_AF, SLOT_AI, SLOT_AG = 0, 1, 2, 3
SLOT_BQ, SLOT_BK, SLOT_BV, SLOT_BG = 5, 6, 7, 8
SLOT_CQK, SLOT_CV, SLOT_CG = 9, 10, 11
C_QK_TILES = 2 * C_QK // LANE
SLOT_KIND = (["silu"] * A_HEADS + ["keep"] * A_HEADS + ["none"] * A_HEADS + ["silu"] * A_HEADS
             + ["none"] * (1 + 3 * B_HEADS) + ["silu"] * B_HEADS
             + ["none"] * (C_QK_TILES + C_HEADS) + ["silu"] * C_HEADS)

TM_PROJ = 1024
SLOTS_PER_STEP = 20
SPARE_SLOT = A_HEADS * 4
TN_SUB = 512
TM_OUT = 512
TA = 2048
A_CHUNK = 64
TB = 256
TC = 256
TBC_STEP = 512
BAND_AHEAD = 2
A_LEVELS = (1, 2, 4, 8, 16, 32)
N_LEV = len(A_LEVELS)
N_BUF = 2
VMEM_LIMIT = 56 * 1024 * 1024
DEC_ROWS = SUBLANE


def _row_vec(v):
    return v.reshape(1, -1)


def _cparams(sem):
    return pltpu.CompilerParams(dimension_semantics=sem, vmem_limit_bytes=VMEM_LIMIT)


def _activate(t, slot, lb_ref):
    kind = SLOT_KIND[slot]
    if kind == "silu":
        half = 0.5 * t
        return half + half * jnp.tanh(half)
    if kind == "keep":
        head = slot - SLOT_AF * A_HEADS
        oml = 1.0 - lb_ref[0:1, head * A_D:(head + 1) * A_D]
        e = jnp.exp(-jnp.abs(t))
        rcp = 1.0 / (1.0 + e)
        return oml * jnp.where(t >= 0.0, e * rcp, rcp)
    return t


def _w_slot_start(j):
    return j * SLOTS_PER_STEP - (j > SPARE_SLOT // SLOTS_PER_STEP)


def _dot_groups(jj):
    first = jj * SLOTS_PER_STEP
    w0 = int(_w_slot_start(jj))
    runs = [(first, first + SLOTS_PER_STEP)]
    if first <= SPARE_SLOT < first + SLOTS_PER_STEP:
        runs = [(first, SPARE_SLOT), (SPARE_SLOT + 1, first + SLOTS_PER_STEP)]
    groups = []
    for lo, hi in runs:
        for s in range(lo, hi, TN_SUB // LANE):
            n = min(TN_SUB // LANE, hi - s)
            groups.append((s, s - (s > SPARE_SLOT) - w0, n))
    return groups


def _project(h_ref, w_ref, lb_ref, o_ref):
    for jj in range(N_SLOTS // SLOTS_PER_STEP):
        @pl.when(pl.program_id(1) == jj)
        def _(jj=jj):
            h = h_ref[...]
            first = jj * SLOTS_PER_STEP
            if first <= SPARE_SLOT < first + SLOTS_PER_STEP:
                o_ref[SPARE_SLOT - first] = jnp.zeros((TM_PROJ, LANE), BF16)
            cost = {"keep": 0, "silu": 1, "none": 2}
            order = lambda g: min(cost[SLOT_KIND[g[0] + k]] for k in range(g[2]))
            for s0, wt, n in sorted(_dot_groups(jj), key=order):
                acc = jnp.dot(h, w_ref[:, wt * LANE:(wt + n) * LANE], preferred_element_type=F32)
                for k in range(n):
                    t = _activate(acc[:, k * LANE:(k + 1) * LANE], s0 + k, lb_ref)
                    o_ref[s0 - first + k] = t.astype(BF16)


def _inproj_kernel(h_ref, w_ref, lb_ref, o_ref):
    _project(h_ref, w_ref, lb_ref, o_ref)


def _inproj_norm_kernel(x_ref, g_ref, w_ref, lb_ref, o_ref, h_ref):
    @pl.when(pl.program_id(1) == 0)
    def _():
        x = x_ref[...]
        ms = jnp.mean(x * x, axis=-1, keepdims=True)
        h_ref[...] = (x * lax.rsqrt(ms + EPS) * g_ref[0:1, :]).astype(BF16)

    _project(h_ref, w_ref, lb_ref, o_ref)


def _inproj(h, w, lb, gain=None):
    m = h.shape[0]
    tm = TM_PROJ
    rows = pl.BlockSpec((tm, D_MODEL), lambda i, j: (i, 0))
    after_spare = SPARE_SLOT // SLOTS_PER_STEP
    wspec = pl.BlockSpec(
        (pl.Element(D_MODEL), pl.Element(SLOTS_PER_STEP * LANE)),
        lambda i, j: (0, (j * SLOTS_PER_STEP - jnp.where(j > after_spare, 1, 0)) * LANE))
    common = dict(
        grid=(m // tm, N_SLOTS // SLOTS_PER_STEP),
        out_specs=pl.BlockSpec((SLOTS_PER_STEP, tm, LANE), lambda i, j: (j, i, 0)),
        out_shape=jax.ShapeDtypeStruct((N_SLOTS, m, LANE), BF16),
        compiler_params=_cparams(("arbitrary", "arbitrary")),
    )
    lbspec = pl.BlockSpec((1, A_WIDTH), lambda i, j: (0, 0))
    lb = _row_vec(lb)
    if gain is None:
        return pl.pallas_call(_inproj_kernel, in_specs=[rows, wspec, lbspec], name="inproj",
                              **common)(h, w, lb)
    return pl.pallas_call(
        _inproj_norm_kernel,
        in_specs=[rows, pl.BlockSpec((1, D_MODEL), lambda i, j: (0, 0)), wspec, lbspec],
        scratch_shapes=[pltpu.VMEM((TM_PROJ, D_MODEL), BF16)],
        name="inproj_norm", **common)(h, _row_vec(gain), w, lb)


def _row(x, r):
    return jnp.broadcast_to(x[r:r + 1, :], (SUBLANE, x.shape[1]))


def _decay_products(f):
    n, w = f.shape
    ng = n // SUBLANE
    r = lax.broadcasted_iota(jnp.int32, (SUBLANE, w), 0)
    b0, b1, b2 = (r & 1) == 1, (r & 2) == 2, (r & 4) == 4
    lo = r < 4
    one = jnp.ones((SUBLANE, w), F32)
    groups = [f[SUBLANE * j:SUBLANE * (j + 1), :] for j in range(ng)]
    p = {1: groups, 2: [], 4: [], 8: []}
    rr = {1: [one] * ng, 2: [], 4: [], 8: []}
    for x1 in groups:
        x2 = x1 * jnp.where(b0, pltpu.roll(x1, 1, 0), one)
        x4 = x2 * jnp.where(b1, jnp.where(lo, _row(x2, 1), _row(x2, 5)), one)
        x8 = x4 * jnp.where(b2, _row(x4, 3), one)
        r2 = jnp.where(b0, one, pltpu.roll(x1, 7, 0))
        r4 = r2 * jnp.where(b1, one, jnp.where(lo, _row(x2, 3), _row(x2, 7)))
        r8 = r4 * jnp.where(b2, one, _row(x4, 7))
        p[2].append(x2); p[4].append(x4); p[8].append(x8)
        rr[2].append(r2); rr[4].append(r4); rr[8].append(r8)
    m = SUBLANE
    while m < n:
        g = m // SUBLANE
        tot = [_row(p[m][g * b + g - 1], SUBLANE - 1) for b in range(n // m)]
        pn, rn = [], []
        for j in range(ng):
            b = j // g
            pn.append(p[m][j] * tot[b - 1] if b % 2 == 1 else p[m][j])
            rn.append(rr[m][j] * tot[b + 1] if b % 2 == 0 else rr[m][j])
        p[2 * m], rr[2 * m] = pn, rn
        m *= 2
    cat = lambda xs: jnp.concatenate(xs, axis=0)
    return {m: (cat(p[m]), cat(rr[m])) for m in p}


def _hgrn_kernel(q_ref, k_ref, i_ref, g_ref, gain_ref, mask_ref, y_ref, st_ref, *buf_refs):
    @pl.when(pl.program_id(1) == 0)
    def _():
        st_ref[...] = jnp.zeros_like(st_ref)

    gain = gain_ref[0:1, :]
    nt = (((1,), (1,)), ((), ()))
    tn = (((0,), (0,)), ((), ()))
    bufs = [buf_refs[3 * b:3 * b + 3] for b in range(N_BUF)]
    n_groups = A_CHUNK // SUBLANE

    def prepare(c, buf):
        lev, aux, dec = buf
        r0 = pl.multiple_of(c * A_CHUNK, A_CHUNK)

        def load(ref):
            return jnp.concatenate(
                [ref[h, pl.ds(r0, A_CHUNK), :] for h in range(A_HEADS)], axis=-1)

        qb, kb = load(q_ref), load(k_ref)
        qs, kk, vv = qb.astype(F32), kb.astype(F32), load(i_ref).astype(F32)
        prod = _decay_products(1.0 - kk)
        for li, m in enumerate(A_LEVELS):
            lev[li] = qb * prod[m][0].astype(BF16)
            lev[N_LEV + li] = kb if m == 1 else kb * prod[m][1].astype(BF16)
        lev[2 * N_LEV] = qb * prod[A_CHUNK][0].astype(BF16)
        lev[2 * N_LEV + 1] = kb * prod[A_CHUNK][1].astype(BF16)
        dec[0:1, :] = prod[A_CHUNK][0][A_CHUNK - 1:A_CHUNK, :]
        qk = qs * kk
        aux[...] = jnp.concatenate(
            [jnp.sum(qk[:, h * A_D:(h + 1) * A_D], axis=-1, keepdims=True)
             * vv[:, h * A_D:(h + 1) * A_D] for h in range(A_HEADS)], axis=-1)

    heads = range(A_HEADS)
    sl = [slice(h * A_D, (h + 1) * A_D) for h in heads]

    def mix_a(c, buf):
        lev, aux, dec = buf
        r0 = pl.multiple_of(c * A_CHUNK, A_CHUNK)
        s_tot = []
        for h in heads:
            rows = [jnp.zeros((SUBLANE, A_CHUNK), F32)] * n_groups
            for li, m in enumerate(A_LEVELS):
                s = lax.dot_general(lev[li, :, sl[h]], lev[N_LEV + li, :, sl[h]], nt,
                                    preferred_element_type=F32)
                for j in range(n_groups):
                    if m < SUBLANE or (SUBLANE * j // m) % 2 == 1:
                        rg = slice(SUBLANE * j, SUBLANE * (j + 1))
                        rows[j] = rows[j] + s[rg, :] * mask_ref[li, rg, :]
            s_tot.append(jnp.concatenate(rows, axis=0).astype(BF16))
        st = [st_ref[h] for h in heads]
        vb = [i_ref[h, pl.ds(r0, A_CHUNK), :] for h in heads]
        o_in = [lax.dot_general(lev[2 * N_LEV, :, sl[h]], st[h].astype(BF16), nt,
                                preferred_element_type=F32) for h in heads]
        for h in heads:
            st_ref[h] = st[h] * dec[0:1, sl[h]] + lax.dot_general(
                vb[h], lev[2 * N_LEV + 1, :, sl[h]], tn, preferred_element_type=F32)
        return s_tot, vb, [o_in[h] + aux[:, sl[h]] for h in heads]

    def mix_b(c, parts):
        s_tot, vb, rest = parts
        r0 = pl.multiple_of(c * A_CHUNK, A_CHUNK)
        for h in heads:
            o = jnp.dot(s_tot[h], vb[h], preferred_element_type=F32) + rest[h]
            o = o * lax.rsqrt(jnp.mean(o * o, axis=-1, keepdims=True) + EPS * A_D) * gain
            y_ref[pl.ds(r0, A_CHUNK), sl[h]] = o.astype(BF16) * g_ref[h, pl.ds(r0, A_CHUNK), :]

    n_chunks = TA // A_CHUNK

    def pair(c, more):
        prepare(c + 1, bufs[1])
        a0 = mix_a(c, bufs[0])
        a1 = mix_a(c + 1, bufs[1])
        mix_b(c, a0)
        mix_b(c + 1, a1)
        if more:
            prepare(c + 2, bufs[0])

    prepare(0, bufs[0])
    lax.fori_loop(0, n_chunks // 2 - 1, lambda k, carry: pair(2 * k, True) or carry, 0)
    pair(n_chunks - 2, False)


def _hgrn_masks():
    t = np.arange(A_CHUNK)[:, None]
    s = np.arange(A_CHUNK)[None, :]
    out = []
    for m in A_LEVELS:
        ok = (t // (2 * m) == s // (2 * m)) & ((t // m) % 2 == 1) & ((s // m) % 2 == 0)
        out.append(ok)
    return np.stack(out).astype(np.float32)


def _hgrn(proj, gain, bsz, seq):
    m = bsz * seq
    nb = seq // TA
    spec = lambda blk: pl.BlockSpec((A_HEADS, TA, LANE), lambda b, i, blk=blk: (blk, b * nb + i, 0))
    return pl.pallas_call(
        _hgrn_kernel,
        grid=(bsz, nb),
        in_specs=[spec(SLOT_AQ), spec(SLOT_AF), spec(SLOT_AI), spec(SLOT_AG),
                  pl.BlockSpec((1, A_D), lambda b, i: (0, 0)),
                  pl.BlockSpec((N_LEV, A_CHUNK, A_CHUNK), lambda b, i: (0, 0, 0))],
        out_specs=pl.BlockSpec((TA, A_WIDTH), lambda b, i: (b * nb + i, 0)),
        out_shape=jax.ShapeDtypeStruct((m, A_WIDTH), BF16),
        scratch_shapes=[pltpu.VMEM((A_HEADS, A_D, A_D), F32)] + N_BUF * [
            pltpu.VMEM((2 * N_LEV + 2, A_CHUNK, A_WIDTH), BF16),
            pltpu.VMEM((A_CHUNK, A_WIDTH), F32),
            pltpu.VMEM((DEC_ROWS, A_WIDTH), F32)],
        compiler_params=_cparams(("arbitrary", "arbitrary")),
        name="hgrn",
    )(proj, proj, proj, proj, _row_vec(gain), jnp.asarray(_hgrn_masks()))


def _bandret_kernel(q_ref, kp2_ref, kp1_ref, kc_ref, vp2_ref, vp1_ref, vc_ref, bg_ref,
                    bias0_ref, bias1_ref, cqk_ref, cv_ref, cg_ref, cos_ref, sin_ref, dmask_ref,
                    qdec_ref, kdec_ref, kin_ref, cdec_ref, yb_ref, yc_ref, st_ref):
    @pl.when(pl.program_id(1) == 0)
    def _():
        st_ref[...] = jnp.zeros_like(st_ref)

    nt = (((1,), (1,)), ((), ()))
    tn = (((0,), (0,)), ((), ()))
    lane = lax.broadcasted_iota(jnp.int32, (TC, LANE), 1)
    first_half = (lane % C_DK) < (C_DK // 2)
    ones = jnp.ones((3 * TB, B_DH), BF16)
    cheads = range(C_HEADS)
    bias_refs = (bias0_ref, bias1_ref)

    def window(prev2, prev1, cur, u, h):
        blocks = [prev2[h], prev1[h]] + [cur[h, v * TB:(v + 1) * TB, :] for v in range(u + 1)]
        return blocks[-3:]

    for u in range(TBC_STEP // TB):
        rows = slice(u * TB, (u + 1) * TB)

        cos = cos_ref[rows, :]
        sin = sin_ref[rows, :]

        def rotary(x):
            swapped = jnp.where(first_half, pltpu.roll(x, LANE - C_DK // 2, 1),
                                pltpu.roll(x, C_DK // 2, 1))
            return x * cos + swapped * sin

        rot = [rotary(cqk_ref[t, rows, :].astype(F32)) for t in range(C_QK_TILES)]
        qrot = rot[:3]
        krot = [pltpu.roll(x, C_DK, 1) for x in rot[2:]]
        qb = [x.astype(BF16) for x in qrot]
        cv = [cv_ref[h, rows, :] for h in cheads]
        cs = [lax.dot_general(qb[h // 2],
                              (krot[(h + 1) // 2] * kin_ref[h:h + 1, :]).astype(BF16), nt,
                              preferred_element_type=F32) for h in cheads]
        st = [st_ref[h] for h in cheads]
        o_in = [jnp.dot((qrot[h // 2] * qdec_ref[h]).astype(BF16), st[h].astype(BF16),
                        preferred_element_type=F32) for h in cheads]
        for h in cheads:
            st_ref[h] = st[h] * cdec_ref[h:h + 1, :] + lax.dot_general(
                (krot[(h + 1) // 2] * kdec_ref[h]).astype(BF16), cv[h], tn,
                preferred_element_type=F32)

        def scores(h):
            q = q_ref[h, rows, :]
            ks = window(kp2_ref, kp1_ref, kc_ref, u, h)
            return [lax.dot_general(q, ks[j], nt, preferred_element_type=F32)
                    + bias_refs[u][0, h, :, j * TB:(j + 1) * TB] for j in range(3)]

        def attend(h, s):
            mx = jnp.max(jnp.maximum(jnp.maximum(s[0], s[1]), s[2]), axis=-1, keepdims=True)
            p = jnp.concatenate([jnp.exp2(sj - mx).astype(BF16) for sj in s], axis=1)
            v = jnp.concatenate(window(vp2_ref, vp1_ref, vc_ref, u, h), axis=0)
            o = jnp.dot(p, jnp.concatenate([v, ones], axis=1), preferred_element_type=F32)
            yb_ref[rows, h * B_DH:(h + 1) * B_DH] = (
                (o[:, :B_DH] / o[:, B_DH:]).astype(BF16) * bg_ref[h, rows, :])

        pending = [scores(h) for h in range(BAND_AHEAD)]
        for h in cheads:
            o = o_in[h] + jnp.dot((cs[h] * dmask_ref[h]).astype(BF16), cv[h],
                                  preferred_element_type=F32)
            o = o * lax.rsqrt(jnp.mean(o * o, axis=-1, keepdims=True) + EPS)
            yc_ref[rows, h * C_DV:(h + 1) * C_DV] = o.astype(BF16) * cg_ref[h, rows, :]
        for h in range(B_HEADS):
            if h + BAND_AHEAD < B_HEADS:
                pending.append(scores(h + BAND_AHEAD))
            attend(h, pending[h])


def _toeplitz(u):
    hh = u.shape[0]
    ext = jnp.concatenate([u[:, ::-1], jnp.zeros((hh, 1), u.dtype)], axis=1)
    win = jnp.tile(ext, (1, CHUNK + 2))[:, :CHUNK * (LANE + 1)]
    win = win.reshape(hh, CHUNK, LANE + 1)[:, :, :CHUNK]
    return win[:, ::-1, :]


def _band_bias(rel_bias):
    rb = rel_bias.astype(F32) * LOG2E
    hh = rb.shape[0]
    last = rb[:, 2 * REL_CLIP:]
    far = jnp.broadcast_to(last[:, :, None], (hh, CHUNK, CHUNK))
    tile = {
        B_PREV: _toeplitz(rb[:, REL_CLIP - 63:REL_CLIP + 64]),
        B_PREV - 1: _toeplitz(rb[:, REL_CLIP + 1:2 * REL_CLIP]),
        B_PREV - 2: _toeplitz(jnp.concatenate(
            [rb[:, REL_CLIP + 65:], jnp.broadcast_to(last, (hh, CHUNK - 1))], axis=1)),
    }
    neg = jnp.full((hh, CHUNK, CHUNK), NEG_INF, F32)
    rows = []
    for cq in range(TB // CHUNK):
        cols = []
        for w in range(3 * TB // CHUNK):
            jj = w - cq
            cols.append(neg if (jj < 0 or jj > B_PREV) else tile.get(jj, far))
        rows.append(jnp.concatenate(cols, axis=2))
    full = jnp.concatenate(rows, axis=1)
    col = np.arange(3 * TB)[None, None, :]
    step0 = jnp.where(col < 2 * TB, NEG_INF, full)
    step1 = jnp.where(col < TB, NEG_INF, full)
    return jnp.stack([step0, step1, full])


def _ret_consts():
    hh = np.arange(C_HEADS, dtype=np.float64)
    log_gamma = np.log1p(-np.exp2(-5.0 - hh))
    pos = np.arange(TC, dtype=np.float64)
    rel = pos[:, None] - pos[None, :]
    dmask = np.where(rel >= 0, np.exp(log_gamma[:, None, None] * np.maximum(rel, 0.0)), 0.0)
    lane = np.arange(LANE)
    head_lanes = np.stack([(lane // C_DK) == (h % 2) for h in range(C_HEADS)]).astype(np.float64)
    qdec = np.exp(log_gamma[:, None] * (pos + 1.0)[None, :])[:, :, None] * head_lanes[:, None, :]
    kdec = (np.exp(log_gamma[:, None] * (TC - 1 - pos)[None, :])[:, :, None]
            * head_lanes[:, None, :] * C_DK ** -0.5)
    kin = head_lanes * C_DK ** -0.5
    cdec = np.broadcast_to(np.exp(log_gamma * TC)[:, None], (C_HEADS, C_DV))
    f = lambda a: jnp.asarray(np.ascontiguousarray(a), dtype=F32)
    return f(dmask), f(qdec), f(kdec), f(kin), f(cdec)


def _rope_tables(seq):
    inv_freq = ROPE_BASE ** (-jnp.linspace(0.0, 1.0, C_DK // 2, dtype=F32))
    ang = jnp.arange(seq, dtype=F32)[:, None] * inv_freq[None, :]
    cos, sin = jnp.cos(ang), jnp.sin(ang)
    cos_t = jnp.tile(cos, (1, LANE // (C_DK // 2)))
    sin_t = jnp.tile(jnp.concatenate([-sin, sin], axis=1), (1, LANE // C_DK))
    return cos_t, sin_t


def _bandret(proj, bias, cos_t, sin_t, consts, bsz, seq):
    assert TB == TC and TBC_STEP == 2 * TB
    m = bsz * seq
    nb = seq // TBC_STEP
    nsub = seq // TB
    dmask, qdec, kdec, kin, cdec = consts
    cur = lambda blk: pl.BlockSpec(
        (B_HEADS, TBC_STEP, LANE), lambda b, i, blk=blk: (blk, b * nb + i, 0))
    prev = lambda blk, d: pl.BlockSpec(
        (B_HEADS, TB, LANE),
        lambda b, i, blk=blk, d=d: (blk, b * nsub + jnp.maximum(2 * i - d, 0), 0))
    bias_of = lambda u: pl.BlockSpec(
        (1, B_HEADS, TB, 3 * TB), lambda b, i, u=u: (jnp.minimum(2 * i + u, 2), 0, 0, 0))
    const = lambda shape: pl.BlockSpec(shape, lambda b, i: (0,) * len(shape))
    rows = lambda width: pl.BlockSpec((TBC_STEP, width), lambda b, i: (b * nb + i, 0))
    return pl.pallas_call(
        _bandret_kernel,
        grid=(bsz, nb),
        in_specs=[cur(SLOT_BQ),
                  prev(SLOT_BK, 2), prev(SLOT_BK, 1), cur(SLOT_BK),
                  prev(SLOT_BV, 2), prev(SLOT_BV, 1), cur(SLOT_BV),
                  cur(SLOT_BG), bias_of(0), bias_of(1),
                  cur(SLOT_CQK), cur(SLOT_CV), cur(SLOT_CG),
                  pl.BlockSpec((TBC_STEP, LANE), lambda b, i: (i, 0)),
                  pl.BlockSpec((TBC_STEP, LANE), lambda b, i: (i, 0)),
                  const(dmask.shape), const(qdec.shape), const(kdec.shape),
                  const(kin.shape), const(cdec.shape)],
        out_specs=[rows(B_WIDTH), rows(C_WIDTH)],
        out_shape=[jax.ShapeDtypeStruct((m, B_WIDTH), BF16),
                   jax.ShapeDtypeStruct((m, C_WIDTH), BF16)],
        scratch_shapes=[pltpu.VMEM((C_HEADS, LANE, C_DV), F32)],
        compiler_params=_cparams(("arbitrary", "arbitrary")),
        name="bandret",
    )(*([proj] * 8), bias, bias, proj, proj, proj, cos_t, sin_t, dmask, qdec, kdec, kin, cdec)


def _outproj_kernel(ya_ref, yb_ref, yc_ref, x_ref, w_ref, g_ref, *out_refs, last):
    y = jnp.concatenate([ya_ref[...], yb_ref[...], yc_ref[...]], axis=-1)
    xn = x_ref[...] + jnp.dot(y, w_ref[...], preferred_element_type=F32)
    ms = jnp.mean(xn * xn, axis=-1, keepdims=True)
    hn = xn * lax.rsqrt(ms + EPS) * g_ref[0:1, :]
    if last:
        out_refs[0][...] = hn
    else:
        out_refs[0][...] = xn
        out_refs[1][...] = hn.astype(BF16)


def _outproj(ya, yb, yc, x2, w, gain, last):
    m = x2.shape[0]
    row = lambda width: pl.BlockSpec((TM_OUT, width), lambda i: (i, 0))
    if last:
        out_specs = row(D_MODEL)
        out_shape = jax.ShapeDtypeStruct((m, D_MODEL), F32)
    else:
        out_specs = [row(D_MODEL), row(D_MODEL)]
        out_shape = [jax.ShapeDtypeStruct((m, D_MODEL), F32),
                     jax.ShapeDtypeStruct((m, D_MODEL), BF16)]
    return pl.pallas_call(
        functools.partial(_outproj_kernel, last=last),
        grid=(m // TM_OUT,),
        in_specs=[row(A_WIDTH), row(B_WIDTH), row(C_WIDTH), row(D_MODEL),
                  pl.BlockSpec((D_MIX, D_MODEL), lambda i: (0, 0)),
                  pl.BlockSpec((1, D_MODEL), lambda i: (0, 0))],
        out_specs=out_specs,
        out_shape=out_shape,
        compiler_params=_cparams(("arbitrary",)),
        name="outproj_last" if last else "outproj",
    )(ya, yb, yc, x2, w, _row_vec(gain))


def _prep_w_in(w):
    scale = np.ones((1, w.shape[1]), np.float32)
    scale[:, 4 * A_WIDTH:4 * A_WIDTH + B_WIDTH] = B_DH ** -0.5 * LOG2E
    return (w * scale).astype(BF16)


def kernel(x, w_in, norm_gain, lb_logits, hgrn_norm_gain, rel_bias, w_out, final_gain):
    bsz, seq, _ = x.shape
    m = bsz * seq
    lb_all = jnp.cumsum(jax.nn.softmax(lb_logits.astype(F32), axis=0), axis=0)
    lb_all = lb_all - lb_all[0:1]
    cos_t, sin_t = _rope_tables(seq)
    ret_consts = _ret_consts()
    x2 = x.reshape(m, D_MODEL)
    h = None
    for layer in range(DEPTH):
        w = _prep_w_in(w_in[layer])
        lb = lb_all[layer]
        proj = _inproj(x2, w, lb, norm_gain[0]) if layer == 0 else _inproj(h, w, lb)
        ya = _hgrn(proj, hgrn_norm_gain[layer], bsz, seq)
        yb, yc = _bandret(proj, _band_bias(rel_bias[layer]), cos_t, sin_t, ret_consts, bsz, seq)
        last = layer == DEPTH - 1
        gain = final_gain if last else norm_gain[layer + 1]
        res = _outproj(ya, yb, yc, x2, w_out[layer].astype(BF16), gain, last)
        if last:
            x2 = res
        else:
            x2, h = res
    return x2.reshape(bsz, seq, D_MODEL)
```

```python
import functools

import numpy as np
import jax
import jax.numpy as jnp
from jax import lax
from jax.experimental import pallas as pl
from jax.experimental.pallas import tpu as pltpu

F32 = jnp.float32
BF16 = jnp.bfloat16

D_MODEL = 2048
DEPTH = 2
CHUNK = 64
EPS = 1e-6
NEG_INF = -1e30
LOG2E = 1.4426950408889634
LANE = 128
SUBLANE = 8

A_HEADS, A_D = 6, 128
A_WIDTH = A_HEADS * A_D
B_HEADS, B_DH = 5, 128
B_WIDTH = B_HEADS * B_DH
B_PREV = 8
REL_CLIP = 128
C_HEADS, C_DK, C_DV = 5, 64, 128
C_QK = C_HEADS * C_DK
C_WIDTH = C_HEADS * C_DV
ROPE_BASE = 10000.0
D_MIX = A_WIDTH + B_WIDTH + C_WIDTH
D_IN_PAD = 7680
N_SLOTS = D_IN_PAD // LANE

SLOT_AQ, SLOT_AF, SLOT_AI, SLOT_AG = 0, 1, 2, 3
SLOT_BQ, SLOT_BK, SLOT_BV, SLOT_BG = 5, 6, 7, 8
SLOT_CQK, SLOT_CV, SLOT_CG = 9, 10, 11
C_QK_TILES = 2 * C_QK // LANE
SLOT_KIND = (["silu"] * A_HEADS + ["keep"] * A_HEADS + ["none"] * A_HEADS + ["silu"] * A_HEADS
             + ["none"] * (1 + 3 * B_HEADS) + ["silu"] * B_HEADS
             + ["none"] * (C_QK_TILES + C_HEADS) + ["silu"] * C_HEADS)

TM_PROJ = 1024
SLOTS_PER_STEP = 20
SPARE_SLOT = A_HEADS * 4
TN_SUB = 512
TM_OUT = 512
TA = 2048
A_CHUNK = 64
TB = 256
TC = 256
TBC_STEP = 512
BAND_AHEAD = 2
RET_AFTER_HEAD = 3
A_LEVELS = (1, 2, 4, 8, 16, 32)
N_LEV = len(A_LEVELS)
N_BUF = 2
VMEM_LIMIT = 56 * 1024 * 1024
DEC_ROWS = SUBLANE


def _row_vec(v):
    return v.reshape(1, -1)


def _cparams(sem):
    return pltpu.CompilerParams(dimension_semantics=sem, vmem_limit_bytes=VMEM_LIMIT)


def _activate(t, slot, lb_ref):
    kind = SLOT_KIND[slot]
    if kind == "silu":
        half = 0.5 * t
        return half + half * jnp.tanh(half)
    if kind == "keep":
        head = slot - SLOT_AF * A_HEADS
        oml = 1.0 - lb_ref[0:1, head * A_D:(head + 1) * A_D]
        e = jnp.exp(-jnp.abs(t))
        rcp = 1.0 / (1.0 + e)
        return oml * jnp.where(t >= 0.0, e * rcp, rcp)
    return t


def _w_slot_start(j):
    return j * SLOTS_PER_STEP - (j > SPARE_SLOT // SLOTS_PER_STEP)


def _dot_groups(jj):
    first = jj * SLOTS_PER_STEP
    w0 = int(_w_slot_start(jj))
    runs = [(first, first + SLOTS_PER_STEP)]
    if first <= SPARE_SLOT < first + SLOTS_PER_STEP:
        runs = [(first, SPARE_SLOT), (SPARE_SLOT + 1, first + SLOTS_PER_STEP)]
    groups = []
    for lo, hi in runs:
        for s in range(lo, hi, TN_SUB // LANE):
            n = min(TN_SUB // LANE, hi - s)
            groups.append((s, s - (s > SPARE_SLOT) - w0, n))
    return groups


def _project(h_ref, w_ref, lb_ref, o_ref):
    for jj in range(N_SLOTS // SLOTS_PER_STEP):
        @pl.when(pl.program_id(1) == jj)
        def _(jj=jj):
            h = h_ref[...]
            first = jj * SLOTS_PER_STEP
            if first <= SPARE_SLOT < first + SLOTS_PER_STEP:
                o_ref[SPARE_SLOT - first] = jnp.zeros((TM_PROJ, LANE), BF16)
            cost = {"keep": 0, "silu": 1, "none": 2}
            order = lambda g: min(cost[SLOT_KIND[g[0] + k]] for k in range(g[2]))
            for s0, wt, n in sorted(_dot_groups(jj), key=order):
                acc = jnp.dot(h, w_ref[:, wt * LANE:(wt + n) * LANE], preferred_element_type=F32)
                for k in range(n):
                    t = _activate(acc[:, k * LANE:(k + 1) * LANE], s0 + k, lb_ref)
                    o_ref[s0 - first + k] = t.astype(BF16)


def _inproj_kernel(h_ref, w_ref, lb_ref, o_ref):
    _project(h_ref, w_ref, lb_ref, o_ref)


def _inproj_norm_kernel(x_ref, g_ref, w_ref, lb_ref, o_ref, h_ref):
    @pl.when(pl.program_id(1) == 0)
    def _():
        x = x_ref[...]
        ms = jnp.mean(x * x, axis=-1, keepdims=True)
        h_ref[...] = (x * lax.rsqrt(ms + EPS) * g_ref[0:1, :]).astype(BF16)

    _project(h_ref, w_ref, lb_ref, o_ref)


def _inproj(h, w, lb, gain=None):
    m = h.shape[0]
    tm = TM_PROJ
    rows = pl.BlockSpec((tm, D_MODEL), lambda i, j: (i, 0))
    after_spare = SPARE_SLOT // SLOTS_PER_STEP
    wspec = pl.BlockSpec(
        (pl.Element(D_MODEL), pl.Element(SLOTS_PER_STEP * LANE)),
        lambda i, j: (0, (j * SLOTS_PER_STEP - jnp.where(j > after_spare, 1, 0)) * LANE))
    common = dict(
        grid=(m // tm, N_SLOTS // SLOTS_PER_STEP),
        out_specs=pl.BlockSpec((SLOTS_PER_STEP, tm, LANE), lambda i, j: (j, i, 0)),
        out_shape=jax.ShapeDtypeStruct((N_SLOTS, m, LANE), BF16),
        compiler_params=_cparams(("arbitrary", "arbitrary")),
    )
    lbspec = pl.BlockSpec((1, A_WIDTH), lambda i, j: (0, 0))
    lb = _row_vec(lb)
    if gain is None:
        return pl.pallas_call(_inproj_kernel, in_specs=[rows, wspec, lbspec], name="inproj",
                              **common)(h, w, lb)
    return pl.pallas_call(
        _inproj_norm_kernel,
        in_specs=[rows, pl.BlockSpec((1, D_MODEL), lambda i, j: (0, 0)), wspec, lbspec],
        scratch_shapes=[pltpu.VMEM((TM_PROJ, D_MODEL), BF16)],
        name="inproj_norm", **common)(h, _row_vec(gain), w, lb)


def _row(x, r):
    return jnp.broadcast_to(x[r:r + 1, :], (SUBLANE, x.shape[1]))


def _decay_products(f):
    n, w = f.shape
    ng = n // SUBLANE
    r = lax.broadcasted_iota(jnp.int32, (SUBLANE, w), 0)
    b0, b1, b2 = (r & 1) == 1, (r & 2) == 2, (r & 4) == 4
    lo = r < 4
    one = jnp.ones((SUBLANE, w), F32)
    groups = [f[SUBLANE * j:SUBLANE * (j + 1), :] for j in range(ng)]
    p = {1: groups, 2: [], 4: [], 8: []}
    rr = {1: [one] * ng, 2: [], 4: [], 8: []}
    for x1 in groups:
        x2 = x1 * jnp.where(b0, pltpu.roll(x1, 1, 0), one)
        x4 = x2 * jnp.where(b1, jnp.where(lo, _row(x2, 1), _row(x2, 5)), one)
        x8 = x4 * jnp.where(b2, _row(x4, 3), one)
        r2 = jnp.where(b0, one, pltpu.roll(x1, 7, 0))
        r4 = r2 * jnp.where(b1, one, jnp.where(lo, _row(x2, 3), _row(x2, 7)))
        r8 = r4 * jnp.where(b2, one, _row(x4, 7))
        p[2].append(x2); p[4].append(x4); p[8].append(x8)
        rr[2].append(r2); rr[4].append(r4); rr[8].append(r8)
    m = SUBLANE
    while m < n:
        g = m // SUBLANE
        tot = [_row(p[m][g * b + g - 1], SUBLANE - 1) for b in range(n // m)]
        pn, rn = [], []
        for j in range(ng):
            b = j // g
            pn.append(p[m][j] * tot[b - 1] if b % 2 == 1 else p[m][j])
            rn.append(rr[m][j] * tot[b + 1] if b % 2 == 0 else rr[m][j])
        p[2 * m], rr[2 * m] = pn, rn
        m *= 2
    cat = lambda xs: jnp.concatenate(xs, axis=0)
    return {m: (cat(p[m]), cat(rr[m])) for m in p}


def _hgrn_kernel(q_ref, k_ref, i_ref, g_ref, gain_ref, mask_ref, y_ref, st_ref, *buf_refs):
    @pl.when(pl.program_id(1) == 0)
    def _():
        st_ref[...] = jnp.zeros_like(st_ref)

    gain = gain_ref[0:1, :]
    nt = (((1,), (1,)), ((), ()))
    tn = (((0,), (0,)), ((), ()))
    bufs = [buf_refs[3 * b:3 * b + 3] for b in range(N_BUF)]
    n_groups = A_CHUNK // SUBLANE

    def prepare(c, buf):
        lev, aux, dec = buf
        r0 = pl.multiple_of(c * A_CHUNK, A_CHUNK)

        def load(ref):
            return jnp.concatenate(
                [ref[h, pl.ds(r0, A_CHUNK), :] for h in range(A_HEADS)], axis=-1)

        qb, kb = load(q_ref), load(k_ref)
        qs, kk, vv = qb.astype(F32), kb.astype(F32), load(i_ref).astype(F32)
        prod = _decay_products(1.0 - kk)
        for li, m in enumerate(A_LEVELS):
            lev[li] = qb * prod[m][0].astype(BF16)
            lev[N_LEV + li] = kb if m == 1 else kb * prod[m][1].astype(BF16)
        lev[2 * N_LEV] = qb * prod[A_CHUNK][0].astype(BF16)
        lev[2 * N_LEV + 1] = kb * prod[A_CHUNK][1].astype(BF16)
        dec[0:1, :] = prod[A_CHUNK][0][A_CHUNK - 1:A_CHUNK, :]
        qk = qs * kk
        aux[...] = jnp.concatenate(
            [jnp.sum(qk[:, h * A_D:(h + 1) * A_D], axis=-1, keepdims=True)
             * vv[:, h * A_D:(h + 1) * A_D] for h in range(A_HEADS)], axis=-1)

    heads = range(A_HEADS)
    sl = [slice(h * A_D, (h + 1) * A_D) for h in heads]

    def mix_a(c, buf):
        lev, aux, dec = buf
        r0 = pl.multiple_of(c * A_CHUNK, A_CHUNK)
        s_tot = []
        for h in heads:
            rows = [jnp.zeros((SUBLANE, A_CHUNK), F32)] * n_groups
            for li, m in enumerate(A_LEVELS):
                s = lax.dot_general(lev[li, :, sl[h]], lev[N_LEV + li, :, sl[h]], nt,
                                    preferred_element_type=F32)
                for j in range(n_groups):
                    if m < SUBLANE or (SUBLANE * j // m) % 2 == 1:
                        rg = slice(SUBLANE * j, SUBLANE * (j + 1))
                        rows[j] = rows[j] + s[rg, :] * mask_ref[li, rg, :]
            s_tot.append(jnp.concatenate(rows, axis=0).astype(BF16))
        st = [st_ref[h] for h in heads]
        vb = [i_ref[h, pl.ds(r0, A_CHUNK), :] for h in heads]
        o_in = [lax.dot_general(lev[2 * N_LEV, :, sl[h]], st[h].astype(BF16), nt,
                                preferred_element_type=F32) for h in heads]
        for h in heads:
            st_ref[h] = st[h] * dec[0:1, sl[h]] + lax.dot_general(
                vb[h], lev[2 * N_LEV + 1, :, sl[h]], tn, preferred_element_type=F32)
        return s_tot, vb, [o_in[h] + aux[:, sl[h]] for h in heads]

    def mix_b(c, parts):
        s_tot, vb, rest = parts
        r0 = pl.multiple_of(c * A_CHUNK, A_CHUNK)
        for h in heads:
            o = jnp.dot(s_tot[h], vb[h], preferred_element_type=F32) + rest[h]
            o = o * lax.rsqrt(jnp.mean(o * o, axis=-1, keepdims=True) + EPS * A_D) * gain
            y_ref[pl.ds(r0, A_CHUNK), sl[h]] = o.astype(BF16) * g_ref[h, pl.ds(r0, A_CHUNK), :]

    n_chunks = TA // A_CHUNK

    def pair(c, more):
        prepare(c + 1, bufs[1])
        a0 = mix_a(c, bufs[0])
        a1 = mix_a(c + 1, bufs[1])
        mix_b(c, a0)
        mix_b(c + 1, a1)
        if more:
            prepare(c + 2, bufs[0])

    prepare(0, bufs[0])
    lax.fori_loop(0, n_chunks // 2 - 1, lambda k, carry: pair(2 * k, True) or carry, 0)
    pair(n_chunks - 2, False)


def _hgrn_masks():
    t = np.arange(A_CHUNK)[:, None]
    s = np.arange(A_CHUNK)[None, :]
    out = []
    for m in A_LEVELS:
        ok = (t // (2 * m) == s // (2 * m)) & ((t // m) % 2 == 1) & ((s // m) % 2 == 0)
        out.append(ok)
    return np.stack(out).astype(np.float32)


def _hgrn(proj, gain, bsz, seq):
    m = bsz * seq
    nb = seq // TA
    spec = lambda blk: pl.BlockSpec((A_HEADS, TA, LANE), lambda b, i, blk=blk: (blk, b * nb + i, 0))
    return pl.pallas_call(
        _hgrn_kernel,
        grid=(bsz, nb),
        in_specs=[spec(SLOT_AQ), spec(SLOT_AF), spec(SLOT_AI), spec(SLOT_AG),
                  pl.BlockSpec((1, A_D), lambda b, i: (0, 0)),
                  pl.BlockSpec((N_LEV, A_CHUNK, A_CHUNK), lambda b, i: (0, 0, 0))],
        out_specs=pl.BlockSpec((TA, A_WIDTH), lambda b, i: (b * nb + i, 0)),
        out_shape=jax.ShapeDtypeStruct((m, A_WIDTH), BF16),
        scratch_shapes=[pltpu.VMEM((A_HEADS, A_D, A_D), F32)] + N_BUF * [
            pltpu.VMEM((2 * N_LEV + 2, A_CHUNK, A_WIDTH), BF16),
            pltpu.VMEM((A_CHUNK, A_WIDTH), F32),
            pltpu.VMEM((DEC_ROWS, A_WIDTH), F32)],
        compiler_params=_cparams(("arbitrary", "arbitrary")),
        name="hgrn",
    )(proj, proj, proj, proj, _row_vec(gain), jnp.asarray(_hgrn_masks()))


def _bandret_kernel(q_ref, kp2_ref, kp1_ref, kc_ref, vp2_ref, vp1_ref, vc_ref, bg_ref,
                    bias0_ref, bias1_ref, cqk_ref, cv_ref, cg_ref, cos_ref, sin_ref, dmask_ref,
                    qdec_ref, kdec_ref, kin_ref, cdec_ref, yb_ref, yc_ref, st_ref):
    @pl.when(pl.program_id(1) == 0)
    def _():
        st_ref[...] = jnp.zeros_like(st_ref)

    nt = (((1,), (1,)), ((), ()))
    tn = (((0,), (0,)), ((), ()))
    lane = lax.broadcasted_iota(jnp.int32, (TC, LANE), 1)
    first_half = (lane % C_DK) < (C_DK // 2)
    ones = jnp.ones((3 * TB, B_DH), BF16)
    cheads = range(C_HEADS)
    bias_refs = (bias0_ref, bias1_ref)

    def window(prev2, prev1, cur, u, h):
        blocks = [prev2[h], prev1[h]] + [cur[h, v * TB:(v + 1) * TB, :] for v in range(u + 1)]
        return blocks[-3:]

    for u in range(TBC_STEP // TB):
        rows = slice(u * TB, (u + 1) * TB)

        cos = cos_ref[rows, :]
        sin = sin_ref[rows, :]

        def rotary(x):
            swapped = jnp.where(first_half, pltpu.roll(x, LANE - C_DK // 2, 1),
                                pltpu.roll(x, C_DK // 2, 1))
            return x * cos + swapped * sin

        rot = [rotary(cqk_ref[t, rows, :].astype(F32)) for t in range(C_QK_TILES)]
        qrot = rot[:3]
        krot = [pltpu.roll(x, C_DK, 1) for x in rot[2:]]
        qb = [x.astype(BF16) for x in qrot]
        cv = [cv_ref[h, rows, :] for h in cheads]
        cs = [lax.dot_general(qb[h // 2],
                              (krot[(h + 1) // 2] * kin_ref[h:h + 1, :]).astype(BF16), nt,
                              preferred_element_type=F32) for h in cheads]
        st = [st_ref[h] for h in cheads]
        o_in = [jnp.dot((qrot[h // 2] * qdec_ref[h]).astype(BF16), st[h].astype(BF16),
                        preferred_element_type=F32) for h in cheads]
        for h in cheads:
            st_ref[h] = st[h] * cdec_ref[h:h + 1, :] + lax.dot_general(
                (krot[(h + 1) // 2] * kdec_ref[h]).astype(BF16), cv[h], tn,
                preferred_element_type=F32)

        def scores(h):
            q = q_ref[h, rows, :]
            ks = window(kp2_ref, kp1_ref, kc_ref, u, h)
            return [lax.dot_general(q, ks[j], nt, preferred_element_type=F32)
                    + bias_refs[u][0, h, :, j * TB:(j + 1) * TB] for j in range(3)]

        def attend(h, s):
            mx = jnp.max(jnp.maximum(jnp.maximum(s[0], s[1]), s[2]), axis=-1, keepdims=True)
            p = jnp.concatenate([jnp.exp2(sj - mx).astype(BF16) for sj in s], axis=1)
            v = jnp.concatenate(window(vp2_ref, vp1_ref, vc_ref, u, h), axis=0)
            o = jnp.dot(p, jnp.concatenate([v, ones], axis=1), preferred_element_type=F32)
            yb_ref[rows, h * B_DH:(h + 1) * B_DH] = (
                (o[:, :B_DH] / o[:, B_DH:]).astype(BF16) * bg_ref[h, rows, :])

        pending = [scores(h) for h in range(BAND_AHEAD)]
        for h in range(B_HEADS):
            if h + BAND_AHEAD < B_HEADS:
                pending.append(scores(h + BAND_AHEAD))
            if h == RET_AFTER_HEAD:
                for hc in cheads:
                    o = o_in[hc] + jnp.dot((cs[hc] * dmask_ref[hc]).astype(BF16), cv[hc],
                                           preferred_element_type=F32)
                    o = o * lax.rsqrt(jnp.mean(o * o, axis=-1, keepdims=True) + EPS)
                    yc_ref[rows, hc * C_DV:(hc + 1) * C_DV] = o.astype(BF16) * cg_ref[hc, rows, :]
            attend(h, pending[h])


def _toeplitz(u):
    hh = u.shape[0]
    ext = jnp.concatenate([u[:, ::-1], jnp.zeros((hh, 1), u.dtype)], axis=1)
    win = jnp.tile(ext, (1, CHUNK + 2))[:, :CHUNK * (LANE + 1)]
    win = win.reshape(hh, CHUNK, LANE + 1)[:, :, :CHUNK]
    return win[:, ::-1, :]


def _band_bias(rel_bias):
    rb = rel_bias.astype(F32) * LOG2E
    hh = rb.shape[0]
    last = rb[:, 2 * REL_CLIP:]
    far = jnp.broadcast_to(last[:, :, None], (hh, CHUNK, CHUNK))
    tile = {
        B_PREV: _toeplitz(rb[:, REL_CLIP - 63:REL_CLIP + 64]),
        B_PREV - 1: _toeplitz(rb[:, REL_CLIP + 1:2 * REL_CLIP]),
        B_PREV - 2: _toeplitz(jnp.concatenate(
            [rb[:, REL_CLIP + 65:], jnp.broadcast_to(last, (hh, CHUNK - 1))], axis=1)),
    }
    neg = jnp.full((hh, CHUNK, CHUNK), NEG_INF, F32)
    rows = []
    for cq in range(TB // CHUNK):
        cols = []
        for w in range(3 * TB // CHUNK):
            jj = w - cq
            cols.append(neg if (jj < 0 or jj > B_PREV) else tile.get(jj, far))
        rows.append(jnp.concatenate(cols, axis=2))
    full = jnp.concatenate(rows, axis=1)
    col = np.arange(3 * TB)[None, None, :]
    step0 = jnp.where(col < 2 * TB, NEG_INF, full)
    step1 = jnp.where(col < TB, NEG_INF, full)
    return jnp.stack([step0, step1, full])


def _ret_consts():
    hh = np.arange(C_HEADS, dtype=np.float64)
    log_gamma = np.log1p(-np.exp2(-5.0 - hh))
    pos = np.arange(TC, dtype=np.float64)
    rel = pos[:, None] - pos[None, :]
    dmask = np.where(rel >= 0, np.exp(log_gamma[:, None, None] * np.maximum(rel, 0.0)), 0.0)
    lane = np.arange(LANE)
    head_lanes = np.stack([(lane // C_DK) == (h % 2) for h in range(C_HEADS)]).astype(np.float64)
    qdec = np.exp(log_gamma[:, None] * (pos + 1.0)[None, :])[:, :, None] * head_lanes[:, None, :]
    kdec = (np.exp(log_gamma[:, None] * (TC - 1 - pos)[None, :])[:, :, None]
            * head_lanes[:, None, :] * C_DK ** -0.5)
    kin = head_lanes * C_DK ** -0.5
    cdec = np.broadcast_to(np.exp(log_gamma * TC)[:, None], (C_HEADS, C_DV))
    f = lambda a: jnp.asarray(np.ascontiguousarray(a), dtype=F32)
    return f(dmask), f(qdec), f(kdec), f(kin), f(cdec)


def _rope_tables(seq):
    inv_freq = ROPE_BASE ** (-jnp.linspace(0.0, 1.0, C_DK // 2, dtype=F32))
    ang = jnp.arange(seq, dtype=F32)[:, None] * inv_freq[None, :]
    cos, sin = jnp.cos(ang), jnp.sin(ang)
    cos_t = jnp.tile(cos, (1, LANE // (C_DK // 2)))
    sin_t = jnp.tile(jnp.concatenate([-sin, sin], axis=1), (1, LANE // C_DK))
    return cos_t, sin_t


def _bandret(proj, bias, cos_t, sin_t, consts, bsz, seq):
    assert TB == TC and TBC_STEP == 2 * TB
    m = bsz * seq
    nb = seq // TBC_STEP
    nsub = seq // TB
    dmask, qdec, kdec, kin, cdec = consts
    cur = lambda blk: pl.BlockSpec(
        (B_HEADS, TBC_STEP, LANE), lambda b, i, blk=blk: (blk, b * nb + i, 0))
    prev = lambda blk, d: pl.BlockSpec(
        (B_HEADS, TB, LANE),
        lambda b, i, blk=blk, d=d: (blk, b * nsub + jnp.maximum(2 * i - d, 0), 0))
    bias_of = lambda u: pl.BlockSpec(
        (1, B_HEADS, TB, 3 * TB), lambda b, i, u=u: (jnp.minimum(2 * i + u, 2), 0, 0, 0))
    const = lambda shape: pl.BlockSpec(shape, lambda b, i: (0,) * len(shape))
    rows = lambda width: pl.BlockSpec((TBC_STEP, width), lambda b, i: (b * nb + i, 0))
    return pl.pallas_call(
        _bandret_kernel,
        grid=(bsz, nb),
        in_specs=[cur(SLOT_BQ),
                  prev(SLOT_BK, 2), prev(SLOT_BK, 1), cur(SLOT_BK),
                  prev(SLOT_BV, 2), prev(SLOT_BV, 1), cur(SLOT_BV),
                  cur(SLOT_BG), bias_of(0), bias_of(1),
                  cur(SLOT_CQK), cur(SLOT_CV), cur(SLOT_CG),
                  pl.BlockSpec((TBC_STEP, LANE), lambda b, i: (i, 0)),
                  pl.BlockSpec((TBC_STEP, LANE), lambda b, i: (i, 0)),
                  const(dmask.shape), const(qdec.shape), const(kdec.shape),
                  const(kin.shape), const(cdec.shape)],
        out_specs=[rows(B_WIDTH), rows(C_WIDTH)],
        out_shape=[jax.ShapeDtypeStruct((m, B_WIDTH), BF16),
                   jax.ShapeDtypeStruct((m, C_WIDTH), BF16)],
        scratch_shapes=[pltpu.VMEM((C_HEADS, LANE, C_DV), F32)],
        compiler_params=_cparams(("arbitrary", "arbitrary")),
        name="bandret",
    )(*([proj] * 8), bias, bias, proj, proj, proj, cos_t, sin_t, dmask, qdec, kdec, kin, cdec)


def _outproj_kernel(ya_ref, yb_ref, yc_ref, x_ref, w_ref, g_ref, *out_refs, last):
    y = jnp.concatenate([ya_ref[...], yb_ref[...], yc_ref[...]], axis=-1)
    xn = x_ref[...] + jnp.dot(y, w_ref[...], preferred_element_type=F32)
    ms = jnp.mean(xn * xn, axis=-1, keepdims=True)
    hn = xn * lax.rsqrt(ms + EPS) * g_ref[0:1, :]
    if last:
        out_refs[0][...] = hn
    else:
        out_refs[0][...] = xn
        out_refs[1][...] = hn.astype(BF16)


def _outproj(ya, yb, yc, x2, w, gain, last):
    m = x2.shape[0]
    row = lambda width: pl.BlockSpec((TM_OUT, width), lambda i: (i, 0))
    if last:
        out_specs = row(D_MODEL)
        out_shape = jax.ShapeDtypeStruct((m, D_MODEL), F32)
    else:
        out_specs = [row(D_MODEL), row(D_MODEL)]
        out_shape = [jax.ShapeDtypeStruct((m, D_MODEL), F32),
                     jax.ShapeDtypeStruct((m, D_MODEL), BF16)]
    return pl.pallas_call(
        functools.partial(_outproj_kernel, last=last),
        grid=(m // TM_OUT,),
        in_specs=[row(A_WIDTH), row(B_WIDTH), row(C_WIDTH), row(D_MODEL),
                  pl.BlockSpec((D_MIX, D_MODEL), lambda i: (0, 0)),
                  pl.BlockSpec((1, D_MODEL), lambda i: (0, 0))],
        out_specs=out_specs,
        out_shape=out_shape,
        compiler_params=_cparams(("arbitrary",)),
        name="outproj_last" if last else "outproj",
    )(ya, yb, yc, x2, w, _row_vec(gain))


def _prep_w_in(w):
    scale = np.ones((1, w.shape[1]), np.float32)
    scale[:, 4 * A_WIDTH:4 * A_WIDTH + B_WIDTH] = B_DH ** -0.5 * LOG2E
    return (w * scale).astype(BF16)


def kernel(x, w_in, norm_gain, lb_logits, hgrn_norm_gain, rel_bias, w_out, final_gain):
    bsz, seq, _ = x.shape
    m = bsz * seq
    lb_all = jnp.cumsum(jax.nn.softmax(lb_logits.astype(F32), axis=0), axis=0)
    lb_all = lb_all - lb_all[0:1]
    cos_t, sin_t = _rope_tables(seq)
    ret_consts = _ret_consts()
    x2 = x.reshape(m, D_MODEL)
    h = None
    for layer in range(DEPTH):
        w = _prep_w_in(w_in[layer])
        lb = lb_all[layer]
        proj = _inproj(x2, w, lb, norm_gain[0]) if layer == 0 else _inproj(h, w, lb)
        ya = _hgrn(proj, hgrn_norm_gain[layer], bsz, seq)
        yb, yc = _bandret(proj, _band_bias(rel_bias[layer]), cos_t, sin_t, ret_consts, bsz, seq)
        last = layer == DEPTH - 1
        gain = final_gain if last else norm_gain[layer + 1]
        res = _outproj(ya, yb, yc, x2, w_out[layer].astype(BF16), gain, last)
        if last:
            x2 = res
        else:
            x2, h = res
    return x2.reshape(bsz, seq, D_MODEL)
```
